```python
import jax
import jax.numpy as jnp
from jax import lax
import numpy as np

D_MODEL = 4096
BATCH = 16
SEQ = 256
DEPTH = 2
DEC_BATCH = 2
DEC_SEQ = 2048
PAST_LEN = 256

GRID_W = 64
POS_BASE = 10000.0
CONV_K = 5
SSM_D_INNER = 2 * D_MODEL
SSM_HEADDIM = 64
SSM_HEADS = SSM_D_INNER // SSM_HEADDIM
SSM_GROUPS = 8
SSM_STATE = 128
SSM_CHUNK = 128
SSM_XBC = SSM_D_INNER + 2 * SSM_GROUPS * SSM_STATE
GDN_DK = 128
GDN_DV = 128
GDN_QK_HEADS = D_MODEL // 128
GDN_V_HEADS = 2 * GDN_QK_HEADS
GDN_KEY_DIM = GDN_QK_HEADS * GDN_DK
GDN_VAL_DIM = GDN_V_HEADS * GDN_DV
GDN_QKV = 2 * GDN_KEY_DIM + GDN_VAL_DIM
GDN_CHUNK = 64
N_EXPERTS = 32
N_GROUPS = 8
GROUP_SIZE = N_EXPERTS // N_GROUPS
TOP_K = 2
D_FF_EXPERT = 1024
MOE_BLOCK = 128
DEEPNORM_ALPHA = (2.0 * DEPTH) ** 0.25
DEEPNORM_BETA = (8.0 * DEPTH) ** -0.25
LN_EPS = 1e-5
RMS_EPS = 1e-6
IN_SIZES = (SSM_D_INNER, SSM_XBC, 2 * SSM_HEADS, GDN_QKV, GDN_VAL_DIM, 2 * GDN_V_HEADS, 2 * GDN_V_HEADS, D_MODEL, D_MODEL)
IN_COLS = int(sum(IN_SIZES))
IN_SPLIT = tuple(np.cumsum(IN_SIZES)[:-1].tolist())

kernel_name = 'bidir_ssd_gdn_moe_diffusion_step'


def layer_norm(x, g, b):
    xf = x.astype(jnp.float32)
    mu = jnp.mean(xf, -1, keepdims=True)
    var = jnp.mean(jnp.square(xf - mu), -1, keepdims=True)
    return ((xf - mu) * lax.rsqrt(var + LN_EPS) * g + b).astype(x.dtype)


def rms_normalize(x):
    xf = x.astype(jnp.float32)
    return xf * lax.rsqrt(jnp.mean(xf * xf, -1, keepdims=True) + RMS_EPS)


def l2_normalize(x):
    return x * lax.rsqrt(jnp.sum(x * x, -1, keepdims=True) + RMS_EPS)


def flip_seq(t):
    return jnp.flip(t, axis=1)


def centred_dwconv(x, w):
    pad = CONV_K // 2
    length = x.shape[1]
    xp = jnp.pad(x, ((0, 0), (pad, pad), (0, 0)))
    out = xp[:, 0:length] * w[0]
    for i in range(1, CONV_K):
        out = out + xp[:, i:i + length] * w[i]
    return out


def grid_pos_embed(rows, d):
    nf = d // 4
    omega = 1.0 / (POS_BASE ** (jnp.arange(nf, dtype=jnp.float32) / nf))
    ang_r = jnp.arange(rows, dtype=jnp.float32)[:, None] * omega
    ang_c = jnp.arange(GRID_W, dtype=jnp.float32)[:, None] * omega
    emb_r = jnp.concatenate([jnp.sin(ang_r), jnp.cos(ang_r)], -1)
    emb_c = jnp.concatenate([jnp.sin(ang_c), jnp.cos(ang_c)], -1)
    emb = jnp.concatenate([jnp.broadcast_to(emb_r[:, None], (rows, GRID_W, d // 2)),
                           jnp.broadcast_to(emb_c[None], (rows, GRID_W, d // 2))], -1)
    return emb.reshape(rows * GRID_W, d)


def ssd_scan(x, dt, a_neg, bm, cm, h0):
    bsz, length, nh, hp = x.shape
    ng, ns = bm.shape[2], bm.shape[3]
    nr = nh // ng
    nc = length // SSM_CHUNK
    q = SSM_CHUNK
    xc = x.reshape(bsz, nc, q, ng, nr, hp)
    dtc = dt.reshape(bsz, nc, q, ng, nr)
    acum = jnp.cumsum(dtc * a_neg.reshape(ng, nr), axis=2)
    bc = bm.reshape(bsz, nc, q, ng, ns)
    cc = cm.reshape(bsz, nc, q, ng, ns)
    causal = jnp.tril(jnp.ones((q, q), dtype=bool))[:, :, None, None]
    seg = acum[:, :, :, None] - acum[:, :, None, :]
    decay = jnp.exp(jnp.where(causal, seg, -jnp.inf))
    cb = jnp.einsum('bcign,bcjgn->bcijg', cc, bc)
    wgt = cb[..., None] * decay * dtc[:, :, None]
    y_diag = jnp.einsum('bcijgr,bcjgrp->bcigrp', wgt, xc)
    to_end = jnp.exp(acum[:, :, -1:] - acum) * dtc
    states = jnp.einsum('bcjgr,bcjgn,bcjgrp->bcgrpn', to_end, bc, xc)
    chunk_decay = jnp.exp(acum[:, :, -1])

    def step(h, inp):
        st, dc = inp
        return h * dc[..., None, None] + st, h

    h_final, h_enter = lax.scan(step, h0.reshape(bsz, ng, nr, hp, ns),
                                (jnp.moveaxis(states, 1, 0), jnp.moveaxis(chunk_decay, 1, 0)))
    h_enter = jnp.moveaxis(h_enter, 0, 1)
    y_off = jnp.einsum('bcign,bcgrpn->bcigrp', cc, h_enter) * jnp.exp(acum)[..., None]
    y = (y_diag + y_off).reshape(bsz, length, nh, hp)
    return y, h_final.reshape(bsz, nh, hp, ns)


def gdn_scan(q, k, v, g, beta, s0):
    bsz, length, nh, dk = q.shape
    dv = v.shape[-1]
    c = GDN_CHUNK
    nc = length // c
    to_chunks = lambda t: t.reshape(bsz, nc, c, nh, t.shape[-1]).transpose(0, 1, 3, 2, 4)
    qc, kc, vc = to_chunks(q), to_chunks(k), to_chunks(v)
    gc = jnp.cumsum(g.reshape(bsz, nc, c, nh).transpose(0, 1, 3, 2), axis=-1)
    bc = beta.reshape(bsz, nc, c, nh).transpose(0, 1, 3, 2)
    incl = jnp.tril(jnp.ones((c, c), dtype=bool))
    strict = jnp.tril(jnp.ones((c, c), dtype=bool), k=-1)
    dec = jnp.exp(jnp.where(incl, gc[..., :, None] - gc[..., None, :], -jnp.inf))
    kb = kc * bc[..., None]
    n_mat = jnp.where(strict, jnp.einsum('bnhik,bnhjk->bnhij', kb, kc) * dec, 0.0)
    rhs = jnp.concatenate([vc * bc[..., None], kb * jnp.exp(gc)[..., None]], axis=-1)
    sol = lax.linalg.triangular_solve(jnp.eye(c, dtype=n_mat.dtype) + n_mat, rhs, left_side=True, lower=True)
    u, w = sol[..., :dv], sol[..., dv:]
    qk = jnp.where(incl, jnp.einsum('bnhik,bnhjk->bnhij', qc, kc) * dec, 0.0)
    g_last = gc[..., -1]
    k_tail = kc * jnp.exp(g_last[..., None] - gc)[..., None]
    q_dec = qc * jnp.exp(gc)[..., None]

    def step(s, inp):
        u_c, w_c, qk_c, qd_c, kt_c, gl_c = inp
        v_new = u_c - jnp.einsum('bhck,bhkv->bhcv', w_c, s)
        o = jnp.einsum('bhck,bhkv->bhcv', qd_c, s) + jnp.einsum('bhij,bhjv->bhiv', qk_c, v_new)
        s = s * jnp.exp(gl_c)[..., None, None] + jnp.einsum('bhck,bhcv->bhkv', kt_c, v_new)
        return s, o

    xs = tuple(jnp.moveaxis(t, 1, 0) for t in (u, w, qk, q_dec, k_tail, g_last))
    s_final, o = lax.scan(step, s0, xs)
    o = o.transpose(1, 0, 3, 2, 4).reshape(bsz, length, nh, dv)
    return o, s_final


def ssd_branch(z, xbc, dt_raw, h0, p, l):
    bsz, length, _ = z.shape
    xbc = jax.nn.silu(centred_dwconv(xbc, p['conv_ssm_w'][l]) + p['conv_ssm_b'][l]).astype(jnp.float32)
    xs, bm, cm = jnp.split(xbc, [SSM_D_INNER, SSM_D_INNER + SSM_GROUPS * SSM_STATE], axis=-1)
    xs = xs.reshape(bsz, length, SSM_HEADS, SSM_HEADDIM)
    bm = bm.reshape(bsz, length, SSM_GROUPS, SSM_STATE)
    cm = cm.reshape(bsz, length, SSM_GROUPS, SSM_STATE)
    dt = jax.nn.softplus(dt_raw.astype(jnp.float32).reshape(bsz, length, 2, SSM_HEADS)
                         + p['ssm_dt_bias'][l].astype(jnp.float32))
    a_neg = -jnp.exp(p['ssm_a_log'][l].astype(jnp.float32))
    if h0 is None:
        h0 = jnp.zeros((bsz, 2, SSM_HEADS, SSM_HEADDIM, SSM_STATE), jnp.float32)
    h0 = h0.astype(jnp.float32)
    y_f, h_f = ssd_scan(xs, dt[:, :, 0], a_neg[0], bm, cm, h0[:, 0])
    y_b, h_b = ssd_scan(flip_seq(xs), flip_seq(dt[:, :, 1]), a_neg[1], flip_seq(bm), flip_seq(cm), h0[:, 1])
    y = y_f + flip_seq(y_b) + p['ssm_d'][l].astype(jnp.float32)[:, None] * xs
    y = y.reshape(bsz, length, SSM_D_INNER) * jax.nn.silu(z.astype(jnp.float32))
    y = rms_normalize(y.reshape(bsz, length, SSM_GROUPS, SSM_D_INNER // SSM_GROUPS)).reshape(bsz, length, SSM_D_INNER)
    y = y * p['ssm_norm_w'][l]
    return y.astype(z.dtype), jnp.stack([h_f, h_b], axis=1)


def gdn_branch(qkv, z, a_raw, b_raw, s0, p, l):
    bsz, length, _ = qkv.shape
    qkv = jax.nn.silu(centred_dwconv(qkv, p['conv_gdn_w'][l])).astype(jnp.float32)
    q, k, v = jnp.split(qkv, [GDN_KEY_DIM, 2 * GDN_KEY_DIM], axis=-1)
    rep = GDN_V_HEADS // GDN_QK_HEADS
    q = jnp.repeat(l2_normalize(q.reshape(bsz, length, GDN_QK_HEADS, GDN_DK)), rep, axis=2) * (GDN_DK ** -0.5)
    k = jnp.repeat(l2_normalize(k.reshape(bsz, length, GDN_QK_HEADS, GDN_DK)), rep, axis=2)
    v = v.reshape(bsz, length, GDN_V_HEADS, GDN_DV)
    g = -jnp.exp(p['gdn_a_log'][l].astype(jnp.float32)) * jax.nn.softplus(
        a_raw.astype(jnp.float32).reshape(bsz, length, 2, GDN_V_HEADS) + p['gdn_dt_bias'][l].astype(jnp.float32))
    beta = jax.nn.sigmoid(b_raw.astype(jnp.float32).reshape(bsz, length, 2, GDN_V_HEADS))
    if s0 is None:
        s0 = jnp.zeros((bsz, 2, GDN_V_HEADS, GDN_DK, GDN_DV), jnp.float32)
    s0 = s0.astype(jnp.float32)
    o_f, s_f = gdn_scan(q, k, v, g[:, :, 0], beta[:, :, 0], s0[:, 0])
    o_b, s_b = gdn_scan(flip_seq(q), flip_seq(k), flip_seq(v), flip_seq(g[:, :, 1]), flip_seq(beta[:, :, 1]), s0[:, 1])
    o = o_f + flip_seq(o_b)
    o = rms_normalize(o) * p['gdn_norm_w'][l] * jax.nn.silu(z.astype(jnp.float32).reshape(bsz, length, GDN_V_HEADS, GDN_DV))
    return o.reshape(bsz, length, GDN_VAL_DIM).astype(z.dtype), jnp.stack([s_f, s_b], axis=1)


def token_mixer(h, ssm0, gdn0, p, l):
    proj = h @ p['w_in'][l]
    z_m, xbc, dt_raw, qkv, z_g, a_raw, b_raw, gate_a, gate_b = jnp.split(proj, IN_SPLIT, axis=-1)
    y_ssm, s_ssm = ssd_branch(z_m, xbc, dt_raw, ssm0, p, l)
    y_gdn, s_gdn = gdn_branch(qkv, z_g, a_raw, b_raw, gdn0, p, l)
    merged = (jax.nn.sigmoid(gate_a) * (y_ssm @ p['w_branch_ssm'][l])
              + jax.nn.sigmoid(gate_b) * (y_gdn @ p['w_branch_gdn'][l]))
    return merged @ p['w_out'][l], s_ssm, s_gdn


def routed_moe(h, p, l):
    bsz, length, d = h.shape
    xt = h.reshape(bsz * length, d)
    n_tok = xt.shape[0]
    scores = jax.nn.sigmoid((xt @ p['router_w']).astype(jnp.float32))
    biased = scores + p['router_b'].astype(jnp.float32)
    group_score = jnp.sum(lax.top_k(biased.reshape(n_tok, N_GROUPS, GROUP_SIZE), 2)[0], axis=-1)
    sel_group = jnp.argmax(group_score, axis=-1)
    in_group = (jnp.arange(N_EXPERTS) // GROUP_SIZE)[None, :] == sel_group[:, None]
    _, top_idx = lax.top_k(jnp.where(in_group, biased, -jnp.inf), TOP_K)
    top_w = jnp.take_along_axis(scores, top_idx, axis=-1)
    top_w = top_w / jnp.sum(top_w, -1, keepdims=True)
    n_slot = n_tok * TOP_K
    flat_e = top_idx.reshape(n_slot).astype(jnp.int32)
    order = jnp.argsort(flat_e).astype(jnp.int32)
    sorted_e = flat_e[order]
    counts = jnp.bincount(flat_e, length=N_EXPERTS).astype(jnp.int32)
    padded = (counts + MOE_BLOCK - 1) // MOE_BLOCK * MOE_BLOCK
    start = jnp.cumsum(counts) - counts
    pend = jnp.cumsum(padded)
    pstart = pend - padded
    dest = (pstart[sorted_e] + jnp.arange(n_slot, dtype=jnp.int32) - start[sorted_e]).astype(jnp.int32)
    n_buf = -(-n_slot // MOE_BLOCK) * MOE_BLOCK + N_EXPERTS * MOE_BLOCK
    n_blk = n_buf // MOE_BLOCK
    buf_tok = jnp.full((n_buf,), n_tok, dtype=jnp.int32).at[dest].set(order // TOP_K)
    x_pad = jnp.concatenate([xt, jnp.zeros((1, d), xt.dtype)], axis=0)
    xb = x_pad[buf_tok].reshape(n_blk, MOE_BLOCK, d)
    blk_e = jnp.minimum(jnp.searchsorted(pend, jnp.arange(n_blk, dtype=jnp.int32) * MOE_BLOCK, side='right'),
                        N_EXPERTS - 1)
    w_g, w_u, w_d = p['w_gate'][l], p['w_up'][l], p['w_down'][l]

    def expert_block(args):
        xblk, e = args
        return (jax.nn.silu(xblk @ w_g[e]) * (xblk @ w_u[e])) @ w_d[e]

    yb = lax.map(expert_block, (xb, blk_e)).reshape(n_buf, d)
    slot_dest = jnp.zeros((n_slot,), jnp.int32).at[order].set(dest)
    y = jnp.sum(yb[slot_dest].reshape(n_tok, TOP_K, d).astype(jnp.float32) * top_w[..., None], axis=1)
    return y.reshape(bsz, length, d).astype(h.dtype)


def trunk(x, cond, ssm_cache, gdn_cache, p):
    ssm_states, gdn_states = [], []
    for l in range(DEPTH):
        mod = jax.nn.silu(cond) @ p['w_ada'][l] + p['b_ada'][l]
        sh1, sc1, g1, sh2, sc2, g2 = jnp.split(mod[:, None, :], 6, axis=-1)
        ssm0 = None if ssm_cache is None else ssm_cache[:, l]
        gdn0 = None if gdn_cache is None else gdn_cache[:, l]
        h = x * (1.0 + sc1) + sh1
        mix, s_ssm, s_gdn = token_mixer(h, ssm0, gdn0, p, l)
        x = layer_norm(DEEPNORM_ALPHA * x + g1 * mix, p['ln1_g'][l], p['ln1_b'][l])
        h = x * (1.0 + sc2) + sh2
        x = layer_norm(DEEPNORM_ALPHA * x + g2 * routed_moe(h, p, l), p['ln2_g'][l], p['ln2_b'][l])
        ssm_states.append(s_ssm.astype(x.dtype))
        gdn_states.append(s_gdn.astype(x.dtype))
    return x, jnp.stack(ssm_states, axis=1), jnp.stack(gdn_states, axis=1)


def _normal(k, shape, scale):
    return scale * jax.random.normal(k, shape, jnp.float32)


def _dt_bias(k, shape):
    u = jax.random.uniform(k, shape, jnp.float32)
    dt = jnp.exp(u * (jnp.log(0.1) - jnp.log(0.001)) + jnp.log(0.001))
    return dt + jnp.log(-jnp.expm1(-dt))


def _a_log(k, shape):
    return jnp.log(jax.random.uniform(k, shape, jnp.float32, 1.0, 16.0))


def setup_inputs(seed: int = 0) -> dict:
    key = jax.random.key(seed)
    k = jax.random.split(key, 31)
    d = D_MODEL
    return {
        'x_prompt': _normal(k[0], (BATCH, SEQ, d), 1.0),
        'x_sample': _normal(k[1], (DEC_BATCH, DEC_SEQ, d), 1.0),
        'state_ssm': _normal(k[2], (DEC_BATCH, DEPTH, 2, SSM_HEADS, SSM_HEADDIM, SSM_STATE), 0.3),
        'state_gdn': _normal(k[3], (DEC_BATCH, DEPTH, 2, GDN_V_HEADS, GDN_DK, GDN_DV), 0.1),
        'c': _normal(k[4], (DEC_BATCH, d), 1.0),
        'c_ctx': _normal(k[5], (d,), 1.0),
        'w_ada': _normal(k[6], (DEPTH, d, 6 * d), 0.5 * d ** -0.5),
        'b_ada': _normal(k[7], (DEPTH, 6 * d), 0.02),
        'w_in': _normal(k[8], (DEPTH, d, IN_COLS), d ** -0.5),
        'conv_ssm_w': _normal(k[9], (DEPTH, CONV_K, SSM_XBC), CONV_K ** -0.5),
        'conv_ssm_b': _normal(k[10], (DEPTH, SSM_XBC), 0.02),
        'ssm_dt_bias': _dt_bias(k[11], (DEPTH, 2, SSM_HEADS)),
        'ssm_a_log': _a_log(k[12], (DEPTH, 2, SSM_HEADS)),
        'ssm_d': 1.0 + _normal(k[13], (DEPTH, SSM_HEADS), 0.1),
        'ssm_norm_w': 1.0 + _normal(k[14], (DEPTH, SSM_D_INNER), 0.1),
        'conv_gdn_w': _normal(k[15], (DEPTH, CONV_K, GDN_QKV), CONV_K ** -0.5),
        'gdn_dt_bias': _dt_bias(k[16], (DEPTH, 2, GDN_V_HEADS)),
        'gdn_a_log': _a_log(k[17], (DEPTH, 2, GDN_V_HEADS)),
        'gdn_norm_w': 1.0 + _normal(k[18], (DEPTH, GDN_DV), 0.1),
        'w_branch_ssm': _normal(k[19], (DEPTH, SSM_D_INNER, d), SSM_D_INNER ** -0.5),
        'w_branch_gdn': _normal(k[20], (DEPTH, GDN_VAL_DIM, d), GDN_VAL_DIM ** -0.5),
        'w_out': _normal(k[21], (DEPTH, d, d), DEEPNORM_BETA * d ** -0.5),
        'ln1_g': 1.0 + _normal(k[22], (DEPTH, d), 0.1),
        'ln1_b': _normal(k[23], (DEPTH, d), 0.02),
        'ln2_g': 1.0 + _normal(k[24], (DEPTH, d), 0.1),
        'ln2_b': _normal(k[25], (DEPTH, d), 0.02),
        'router_w': _normal(k[26], (d, N_EXPERTS), d ** -0.5),
        'router_b': _normal(k[27], (N_EXPERTS,), 0.01),
        'w_gate': _normal(k[28], (DEPTH, N_EXPERTS, d, D_FF_EXPERT), d ** -0.5),
        'w_up': _normal(k[29], (DEPTH, N_EXPERTS, d, D_FF_EXPERT), d ** -0.5),
        'w_down': _normal(k[30], (DEPTH, N_EXPERTS, D_FF_EXPERT, d), DEEPNORM_BETA * D_FF_EXPERT ** -0.5),
    }


def reference(x_prompt, x_sample, state_ssm, state_gdn, c, c_ctx, w_ada, b_ada, w_in, conv_ssm_w, conv_ssm_b,
              ssm_dt_bias, ssm_a_log, ssm_d, ssm_norm_w, conv_gdn_w, gdn_dt_bias, gdn_a_log, gdn_norm_w,
              w_branch_ssm, w_branch_gdn, w_out, ln1_g, ln1_b, ln2_g, ln2_b, router_w, router_b,
              w_gate, w_up, w_down):
    p = dict(w_ada=w_ada, b_ada=b_ada, w_in=w_in, conv_ssm_w=conv_ssm_w, conv_ssm_b=conv_ssm_b,
             ssm_dt_bias=ssm_dt_bias, ssm_a_log=ssm_a_log, ssm_d=ssm_d, ssm_norm_w=ssm_norm_w,
             conv_gdn_w=conv_gdn_w, gdn_dt_bias=gdn_dt_bias, gdn_a_log=gdn_a_log, gdn_norm_w=gdn_norm_w,
             w_branch_ssm=w_branch_ssm, w_branch_gdn=w_branch_gdn, w_out=w_out,
             ln1_g=ln1_g, ln1_b=ln1_b, ln2_g=ln2_g, ln2_b=ln2_b,
             router_w=router_w, router_b=router_b, w_gate=w_gate, w_up=w_up, w_down=w_down)
    y_prompt, new_state_ssm, new_state_gdn = trunk(x_prompt, c_ctx[None, :], None, None, p)
    rows = x_sample.shape[1] // GRID_W
    x_lat = x_sample + grid_pos_embed(rows, x_sample.shape[-1]).astype(x_sample.dtype)[None]
    y_sample, _, _ = trunk(x_lat, c, state_ssm, state_gdn, p)
    return (y_prompt, y_sample, new_state_ssm, new_state_gdn)
```

```python
import functools
import math
from typing import NamedTuple

import jax
import jax.numpy as jnp
from jax import lax
from jax.experimental import pallas as pl
from jax.experimental.pallas import tpu as pltpu

F32 = jnp.float32
BF16 = jnp.bfloat16
HI = lax.Precision.HIGHEST

CONV_K = 5
SSM_HEADDIM = 64
SSM_GROUPS = 8
GDN_DK = 128
GDN_DV = 128
ROUTER_GROUPS = 8
TOP_K = 2
GRID_W = 64
POS_BASE = 10000.0
LN_EPS = 1e-5
RMS_EPS = 1e-6

LANES = 128
SUBLANES = 8
VMEM_LIMIT = 56 * 1024 * 1024

SCAN_CHUNK = 128
MOE_ROWS = 128


class Dims(NamedTuple):
    d: int
    depth: int
    nb: int
    seq: int
    db: int
    dseq: int
    tp: int
    t: int
    inner: int
    heads: int
    hpg: int
    gw: int
    state: int
    xbc: int
    qkh: int
    vh: int
    qkv: int
    val: int
    n_exp: int
    dff: int
    in_cols: int
    off_xbc: int
    off_dt: int
    off_qkv: int
    off_zg: int
    off_a: int
    off_b: int
    off_ga: int
    off_gb: int
    ncond: int


def _dims(x_prompt, x_sample, state_ssm, w_in, w_gate):
    nb, seq, d = x_prompt.shape
    db, dseq, _ = x_sample.shape
    depth = w_in.shape[0]
    inner = 2 * d
    heads = inner // SSM_HEADDIM
    state = state_ssm.shape[-1]
    xbc = inner + 2 * SSM_GROUPS * state
    qkh = d // GDN_DK
    vh = 2 * qkh
    key_dim = qkh * GDN_DK
    val = vh * GDN_DV
    qkv = 2 * key_dim + val
    off_xbc = inner
    off_dt = off_xbc + xbc
    off_qkv = off_dt + 2 * heads
    off_zg = off_qkv + qkv
    off_a = off_zg + val
    off_b = off_a + 2 * vh
    off_ga = off_b + 2 * vh
    off_gb = off_ga + d
    in_cols = off_gb + d
    assert in_cols == w_in.shape[2]
    ncond = -(-(1 + db) // SUBLANES) * SUBLANES
    return Dims(d=d, depth=depth, nb=nb, seq=seq, db=db, dseq=dseq, tp=nb * seq, t=nb * seq + db * dseq,
                inner=inner, heads=heads, hpg=heads // SSM_GROUPS, gw=inner // SSM_GROUPS, state=state, xbc=xbc,
                qkh=qkh, vh=vh, qkv=qkv, val=val, n_exp=w_gate.shape[1], dff=w_gate.shape[3], in_cols=in_cols,
                off_xbc=off_xbc, off_dt=off_dt, off_qkv=off_qkv, off_zg=off_zg, off_a=off_a, off_b=off_b,
                off_ga=off_ga, off_gb=off_gb, ncond=ncond)


def _tile(pref, *sizes):
    g = 0
    for s in sizes:
        g = math.gcd(g, s)
    t = math.gcd(pref, g)
    return t


def _params(sem, vmem=VMEM_LIMIT):
    return pltpu.CompilerParams(dimension_semantics=sem, vmem_limit_bytes=vmem)


def _sigmoid(x):
    return 1.0 / (1.0 + jnp.exp(-x))


def _silu(x):
    return x * _sigmoid(x)


def _softplus(x):
    return jnp.maximum(x, 0.0) + jnp.log(1.0 + jnp.exp(-jnp.abs(x)))


def _cond_of_row(dm, row0):
    return jnp.where(row0 < dm.tp, 0, 1 + (row0 - dm.tp) // dm.dseq)


def _ada_kernel(c_ref, w_ref, b_ref, o_ref):
    s = _silu(c_ref[...]).astype(BF16)
    o_ref[...] = jnp.dot(s, w_ref[...].astype(BF16), preferred_element_type=F32) + b_ref[...]


def _ada_mod(dm, cond, w_ada, b_ada):
    n = w_ada.shape[2]
    tn = _tile(512, n)
    return pl.pallas_call(
        _ada_kernel,
        grid=(dm.depth, n // tn),
        in_specs=[pl.BlockSpec((dm.ncond, dm.d), lambda l, j: (0, 0)),
                  pl.BlockSpec((None, dm.d, tn), lambda l, j: (l, 0, j)),
                  pl.BlockSpec((None, 1, tn), lambda l, j: (l, 0, j))],
        out_specs=pl.BlockSpec((None, dm.ncond, tn), lambda l, j: (l, 0, j)),
        out_shape=jax.ShapeDtypeStruct((dm.depth, dm.ncond, n), F32),
        compiler_params=_params(("arbitrary", "arbitrary")),
        name="ada_mod",
    )(cond, w_ada, b_ada.reshape(dm.depth, 1, n))


def _mod_spec(dm, tm, l, which, width=None, col_of=None):
    width = dm.d if width is None else width
    per = dm.d // width

    def idx(*g):
        i, j = (g[0], 0) if col_of is None else col_of(*g)
        return (l, _cond_of_row(dm, i * tm), 0, which * per + j)

    return pl.BlockSpec((None, None, 1, width), idx)


def _embed_kernel(xp_ref, xs_ref, pos_ref, sh_ref, sc_ref, x_ref, h_ref, *, npb):
    is_ctx = pl.program_id(0) < npb
    x = jnp.where(is_ctx, xp_ref[...], xs_ref[...] + pos_ref[...])
    x_ref[...] = x
    h_ref[...] = (x * (1.0 + sc_ref[...]) + sh_ref[...]).astype(BF16)


def _embed(dm, xp, xs, pos, mod4):
    tm = _tile(256, dm.seq, dm.dseq)
    npb = dm.tp // tm
    nsb = (dm.t - dm.tp) // tm
    ppb = dm.dseq // tm
    row = lambda i: (i, 0)
    return pl.pallas_call(
        functools.partial(_embed_kernel, npb=npb),
        grid=(dm.t // tm,),
        in_specs=[pl.BlockSpec((tm, dm.d), lambda i: (jnp.minimum(i, npb - 1), 0)),
                  pl.BlockSpec((tm, dm.d), lambda i: (jnp.clip(i - npb, 0, nsb - 1), 0)),
                  pl.BlockSpec((tm, dm.d), lambda i: (jnp.maximum(i - npb, 0) % ppb, 0)),
                  _mod_spec(dm, tm, 0, 0), _mod_spec(dm, tm, 0, 1)],
        out_specs=[pl.BlockSpec((tm, dm.d), row), pl.BlockSpec((tm, dm.d), row)],
        out_shape=[jax.ShapeDtypeStruct((dm.t, dm.d), F32), jax.ShapeDtypeStruct((dm.t, dm.d), BF16)],
        compiler_params=_params(("arbitrary",)),
        name="embed_mod",
    )(xp, xs, pos, mod4, mod4)


def _mm_kernel(x_ref, w_ref, o_ref, wb_ref):
    @pl.when(pl.program_id(1) == 0)
    def _():
        wb_ref[...] = w_ref[...].astype(BF16)

    o_ref[...] = jnp.dot(x_ref[...], wb_ref[...], preferred_element_type=F32).astype(o_ref.dtype)


def _mm_res_kernel(x_ref, w_ref, r_ref, g_ref, o_ref, wb_ref, *, alpha):
    @pl.when(pl.program_id(1) == 0)
    def _():
        wb_ref[...] = w_ref[...].astype(BF16)

    acc = jnp.dot(x_ref[...], wb_ref[...], preferred_element_type=F32)
    o_ref[...] = alpha * r_ref[...] + g_ref[...] * acc


def _in_proj(dm, h, w_in, l):
    k = dm.d
    n = dm.in_cols
    tm = _tile(1024, dm.seq, dm.dseq)
    tn = _tile(512, n)
    return pl.pallas_call(
        _mm_kernel,
        grid=(n // tn, dm.t // tm),
        in_specs=[pl.BlockSpec((tm, k), lambda j, i: (i, 0)),
                  pl.BlockSpec((None, k, tn), lambda j, i: (l, 0, j))],
        out_specs=pl.BlockSpec((tm, tn), lambda j, i: (i, j)),
        out_shape=jax.ShapeDtypeStruct((dm.t, n), F32),
        scratch_shapes=[pltpu.VMEM((k, tn), BF16)],
        compiler_params=_params(("arbitrary", "arbitrary")),
        name="in_proj",
    )(h, w_in)


def _out_proj(dm, merged, w_out, x, mod4, l, alpha):
    k = dm.d
    tm = _tile(1024, dm.seq, dm.dseq)
    tn = _tile(512, dm.d)
    return pl.pallas_call(
        functools.partial(_mm_res_kernel, alpha=alpha),
        grid=(dm.d // tn, dm.t // tm),
        in_specs=[pl.BlockSpec((tm, k), lambda j, i: (i, 0)),
                  pl.BlockSpec((None, k, tn), lambda j, i: (l, 0, j)),
                  pl.BlockSpec((tm, tn), lambda j, i: (i, j)),
                  _mod_spec(dm, tm, l, 2, width=tn, col_of=lambda j, i: (i, j))],
        out_specs=pl.BlockSpec((tm, tn), lambda j, i: (i, j)),
        out_shape=jax.ShapeDtypeStruct((dm.t, dm.d), F32),
        scratch_shapes=[pltpu.VMEM((k, tn), BF16)],
        compiler_params=_params(("arbitrary", "arbitrary")),
        name="out_proj",
    )(merged, w_out, x, mod4)


def _branch_kernel(ya_ref, yb_ref, wa_ref, wb_ref, ga_ref, gb_ref, o_ref, acc_ref, res_ref, *, nk):
    kb = pl.program_id(2)

    @pl.when(kb % nk == 0)
    def _():
        acc_ref[...] = jnp.zeros_like(acc_ref)

    @pl.when(kb < nk)
    def _():
        acc_ref[...] += jnp.dot(ya_ref[...], wa_ref[...].astype(BF16), preferred_element_type=F32)

    @pl.when(kb >= nk)
    def _():
        acc_ref[...] += jnp.dot(yb_ref[...], wb_ref[...].astype(BF16), preferred_element_type=F32)

    @pl.when(kb == nk - 1)
    def _():
        res_ref[...] = _sigmoid(ga_ref[...]) * acc_ref[...]

    @pl.when(kb == 2 * nk - 1)
    def _():
        o_ref[...] = (res_ref[...] + _sigmoid(gb_ref[...]) * acc_ref[...]).astype(o_ref.dtype)


def _branch_merge(dm, y_ssm, y_gdn, w_a, w_b, proj, l):
    assert dm.inner == dm.val
    kdim = dm.inner
    tm = _tile(1024, dm.seq, dm.dseq)
    tn = _tile(512, dm.d, dm.off_ga, dm.off_gb)
    tk = _tile(2048, kdim)
    nk = kdim // tk
    ca, cb = dm.off_ga // tn, dm.off_gb // tn
    return pl.pallas_call(
        functools.partial(_branch_kernel, nk=nk),
        grid=(dm.d // tn, dm.t // tm, 2 * nk),
        in_specs=[pl.BlockSpec((tm, tk), lambda j, i, kb: (i, jnp.minimum(kb, nk - 1))),
                  pl.BlockSpec((tm, tk), lambda j, i, kb: (i, jnp.maximum(kb - nk, 0))),
                  pl.BlockSpec((None, tk, tn), lambda j, i, kb: (l, jnp.minimum(kb, nk - 1), j)),
                  pl.BlockSpec((None, tk, tn), lambda j, i, kb: (l, jnp.maximum(kb - nk, 0), j)),
                  pl.BlockSpec((tm, tn), lambda j, i, kb: (i, ca + j)),
                  pl.BlockSpec((tm, tn), lambda j, i, kb: (i, cb + j))],
        out_specs=pl.BlockSpec((tm, tn), lambda j, i, kb: (i, j)),
        out_shape=jax.ShapeDtypeStruct((dm.t, dm.d), BF16),
        scratch_shapes=[pltpu.VMEM((tm, tn), F32), pltpu.VMEM((tm, tn), F32)],
        compiler_params=_params(("arbitrary", "arbitrary", "arbitrary")),
        name="branch_merge",
    )(y_ssm, y_gdn, w_a, w_b, proj, proj)


def _gates_kernel(dtr_ref, ar_ref, br_ref, dtb_ref, gb_ref, al_ref, dt_ref, g_ref, beta_ref):
    dt_ref[...] = _softplus(dtr_ref[...] + dtb_ref[...])
    g_ref[...] = -jnp.exp(al_ref[...]) * _softplus(ar_ref[...] + gb_ref[...])
    beta_ref[...] = _sigmoid(br_ref[...])


def _gates(dm, dt_raw, a_raw, b_raw, dt_bias, g_bias, a_log):
    tm = _tile(512, dm.t)
    w1, w2 = 2 * dm.heads, 2 * dm.vh
    row = lambda i: (i, 0)
    fix = lambda i: (0, 0)
    return pl.pallas_call(
        _gates_kernel,
        grid=(dm.t // tm,),
        in_specs=[pl.BlockSpec((tm, w1), row), pl.BlockSpec((tm, w2), row), pl.BlockSpec((tm, w2), row),
                  pl.BlockSpec((1, w1), fix), pl.BlockSpec((1, w2), fix), pl.BlockSpec((1, w2), fix)],
        out_specs=[pl.BlockSpec((tm, w1), row), pl.BlockSpec((tm, w2), row), pl.BlockSpec((tm, w2), row)],
        out_shape=[jax.ShapeDtypeStruct((dm.t, w1), F32), jax.ShapeDtypeStruct((dm.t, w2), F32),
                   jax.ShapeDtypeStruct((dm.t, w2), F32)],
        compiler_params=_params(("arbitrary",)),
        name="gate_prep",
    )(dt_raw, a_raw, b_raw, dt_bias.reshape(1, w1), g_bias.reshape(1, w2), a_log.reshape(1, w2))


def _conv_kernel(*refs, rows, blocks_ctx, blocks_per_ctx_seq, blocks_per_lat_seq, has_bias):
    if has_bias:
        x_ref, prev_ref, next_ref, w_ref, b_ref, o_ref = refs
    else:
        x_ref, prev_ref, next_ref, w_ref, o_ref = refs
        b_ref = None
    i = pl.program_id(0)
    in_ctx = i < blocks_ctx
    pos = jnp.where(in_ctx, i % blocks_per_ctx_seq, (i - blocks_ctx) % blocks_per_lat_seq)
    per = jnp.where(in_ctx, blocks_per_ctx_seq, blocks_per_lat_seq)
    prev = jnp.where(pos == 0, 0.0, prev_ref[...])
    nxt = jnp.where(pos == per - 1, 0.0, next_ref[...])
    ext = jnp.concatenate([prev, x_ref[...], nxt], axis=0)
    n = rows + 2 * SUBLANES
    pad = CONV_K // 2
    w = w_ref[...]
    acc = None
    for tap in range(CONV_K):
        start = SUBLANES - pad + tap
        shifted = ext if start == 0 else pltpu.roll(ext, n - start, 0)
        term = shifted[:rows] * w[tap:tap + 1, :]
        acc = term if acc is None else acc + term
    if has_bias:
        acc = acc + b_ref[...]
    o_ref[...] = _silu(acc)


def _conv_act(dm, proj, col_off, width, w, b, l):
    rows = _tile(256, dm.seq, dm.dseq)
    tc = _tile(512, col_off, width)
    c0 = col_off // tc
    hb = rows // SUBLANES
    nhalo = dm.t // SUBLANES
    has_bias = b is not None
    in_specs = [pl.BlockSpec((rows, tc), lambda i, j: (i, c0 + j)),
                pl.BlockSpec((SUBLANES, tc), lambda i, j: (jnp.maximum(i * hb - 1, 0), c0 + j)),
                pl.BlockSpec((SUBLANES, tc), lambda i, j: (jnp.minimum((i + 1) * hb, nhalo - 1), c0 + j)),
                pl.BlockSpec((None, CONV_K, tc), lambda i, j: (l, 0, j))]
    args = [proj, proj, proj, w]
    if has_bias:
        in_specs.append(pl.BlockSpec((None, 1, tc), lambda i, j: (l, 0, j)))
        args.append(b.reshape(dm.depth, 1, width))
    return pl.pallas_call(
        functools.partial(_conv_kernel, rows=rows, blocks_ctx=dm.tp // rows, blocks_per_ctx_seq=dm.seq // rows,
                          blocks_per_lat_seq=dm.dseq // rows, has_bias=has_bias),
        grid=(dm.t // rows, width // tc),
        in_specs=in_specs,
        out_specs=pl.BlockSpec((rows, tc), lambda i, j: (i, j)),
        out_shape=jax.ShapeDtypeStruct((dm.t, width), F32),
        compiler_params=_params(("arbitrary", "arbitrary")),
        name="conv_act",
    )(*args)


def _ssd_kernel(*refs, nc, hpg, hd, has_state, want_state):
    it = iter(refs)
    xs_ref, b_ref, c_ref, dtc_ref, dtr_ref, alc_ref, alr_ref, z_ref, dsk_ref, nw_ref = (next(it) for _ in range(10))
    h0_ref = next(it) if has_state else None
    y_ref = next(it)
    hout_ref = next(it) if want_state else None
    ht_ref, yf_ref, xw_ref, yc_ref = (next(it) for _ in range(4))

    d = pl.program_id(2)
    c = pl.program_id(3)
    q = xs_ref.shape[0]
    fwd = d == 0
    ii = lax.broadcasted_iota(jnp.int32, (q, q), 0)
    jj = lax.broadcasted_iota(jnp.int32, (q, q), 1)
    mask = ((ii - jj) * (1 - 2 * d)) >= 0
    tri = mask.astype(F32)

    dt_c = dtc_ref[...]
    a_c = dt_c * (-jnp.exp(alc_ref[...]))
    acum_c = jnp.dot(tri, a_c, precision=HI, preferred_element_type=F32)
    dt_r = dtr_ref[...]
    a_r = dt_r * (-jnp.exp(alr_ref[...]))
    acum_r = lax.dot_general(a_r, tri, (((1,), (1,)), ((), ())), precision=HI, preferred_element_type=F32)
    alast_c = jnp.where(fwd, acum_c[q - 1:q, :], acum_c[0:1, :])
    alast_r = jnp.where(fwd, acum_r[:, q - 1:q], acum_r[:, 0:1])

    bm = b_ref[...].astype(BF16)
    cm = c_ref[...].astype(BF16)
    cb = lax.dot_general(cm, bm, (((1,), (1,)), ((), ())), preferred_element_type=F32)

    @pl.when(c == 0)
    def _():
        if has_state:
            ht_ref[...] = h0_ref[...].T
        else:
            ht_ref[...] = jnp.zeros_like(ht_ref)

    ht = ht_ref[...]
    yo = jnp.dot(cm, ht.astype(BF16), preferred_element_type=F32)
    xs = xs_ref[...]
    for r in range(hpg):
        sl = slice(r * hd, (r + 1) * hd)
        ac = acum_c[:, r:r + 1]
        ar = acum_r[r:r + 1, :]
        dec = jnp.where(mask, jnp.exp(ac - ar), 0.0)
        wgt = (cb * dec * dt_r[r:r + 1, :]).astype(BF16)
        x_r = xs[:, sl]
        yd = jnp.dot(wgt, x_r.astype(BF16), preferred_element_type=F32)
        yc_ref[:, sl] = yd + yo[:, sl] * jnp.exp(ac)
        xw_ref[:, sl] = (x_r * (jnp.exp(alast_c[:, r:r + 1] - ac) * dt_c[:, r:r + 1])).astype(BF16)
    st = lax.dot_general(bm, xw_ref[...], (((0,), (0,)), ((), ())), preferred_element_type=F32)
    for r in range(hpg):
        sl = slice(r * hd, (r + 1) * hd)
        ht_ref[:, sl] = ht[:, sl] * jnp.exp(alast_r[r:r + 1, :]) + st[:, sl]

    cidx = c + d * (nc - 1 - 2 * c)
    rows = pl.ds(pl.multiple_of(cidx * q, q), q)

    @pl.when(fwd)
    def _():
        yf_ref[rows, :] = yc_ref[...]

    @pl.when(d == 1)
    def _():
        tot = yf_ref[rows, :] + yc_ref[...] + dsk_ref[...] * xs
        gated = tot * _silu(z_ref[...])
        ms = jnp.mean(gated * gated, axis=-1, keepdims=True)
        y_ref[...] = (gated * lax.rsqrt(ms + RMS_EPS) * nw_ref[...]).astype(y_ref.dtype)

    if want_state:
        @pl.when(c == nc - 1)
        def _():
            hout_ref[...] = ht_ref[...].T


def _ssd_call(dm, act, proj, dtc, dtr, alc, alr, dskip, nw, y_prev, h0, l, row_off, nseq, length):
    q = min(SCAN_CHUNK, length)
    nc = length // q
    base = row_off // q
    g_n, gw, n = SSM_GROUPS, dm.gw, dm.state
    has_state = h0 is not None
    want_state = not has_state
    bcol = dm.inner // n

    def rb(b, d, c):
        return base + b * nc + c + d * (nc - 1 - 2 * c)

    def rb_out(b, d, c):
        return base + b * nc + jnp.where(d == 0, nc - 1, nc - 1 - c)

    in_specs = [pl.BlockSpec((q, gw), lambda b, g, d, c: (rb(b, d, c), g)),
                pl.BlockSpec((q, n), lambda b, g, d, c: (rb(b, d, c), bcol + g)),
                pl.BlockSpec((q, n), lambda b, g, d, c: (rb(b, d, c), bcol + g_n + g)),
                pl.BlockSpec((None, None, q, LANES), lambda b, g, d, c: (d, g, rb(b, d, c), 0)),
                pl.BlockSpec((None, None, dm.hpg, q), lambda b, g, d, c: (d, g, 0, rb(b, d, c))),
                pl.BlockSpec((None, None, 1, LANES), lambda b, g, d, c: (d, g, 0, 0)),
                pl.BlockSpec((None, None, dm.hpg, 1), lambda b, g, d, c: (d, g, 0, 0)),
                pl.BlockSpec((q, gw), lambda b, g, d, c: (rb_out(b, d, c), g)),
                pl.BlockSpec((1, gw), lambda b, g, d, c: (0, g)),
                pl.BlockSpec((1, gw), lambda b, g, d, c: (0, g))]
    args = [act, act, act, dtc, dtr, alc, alr, proj, dskip, nw]
    if has_state:
        in_specs.append(pl.BlockSpec((None, None, None, None, gw, n), lambda b, g, d, c: (b, l, d, g, 0, 0)))
        args.append(h0)
    out_specs = [pl.BlockSpec((q, gw), lambda b, g, d, c: (rb_out(b, d, c), g))]
    out_shape = [jax.ShapeDtypeStruct((dm.t, dm.inner), BF16)]
    if want_state:
        out_specs.append(pl.BlockSpec((None, None, None, gw, n), lambda b, g, d, c: (b, d, g, 0, 0)))
        out_shape.append(jax.ShapeDtypeStruct((nseq, 2, g_n, gw, n), F32))
    aliases = {}
    if y_prev is not None:
        in_specs.append(pl.BlockSpec(memory_space=pl.ANY))
        args.append(y_prev)
        aliases = {len(args) - 1: 0}

    def body(*refs):
        refs = list(refs)
        if y_prev is not None:
            n_in = len(args)
            del refs[n_in - 1]
        _ssd_kernel(*refs, nc=nc, hpg=dm.hpg, hd=SSM_HEADDIM, has_state=has_state, want_state=want_state)

    return pl.pallas_call(
        body,
        grid=(nseq, g_n, 2, nc),
        in_specs=in_specs,
        out_specs=out_specs,
        out_shape=out_shape,
        scratch_shapes=[pltpu.VMEM((n, gw), F32), pltpu.VMEM((length, gw), F32),
                        pltpu.VMEM((q, gw), BF16), pltpu.VMEM((q, gw), F32)],
        input_output_aliases=aliases,
        compiler_params=_params(("arbitrary",) * 4),
        name="ssd_scan",
    )(*args)


def _gdn_kernel(*refs, nc, cl, has_state, want_state):
    it = iter(refs)
    q_ref, k_ref, v0_ref, v1_ref, z0_ref, z1_ref, gcol_ref, grow_ref, nw_ref = (next(it) for _ in range(9))
    s0_ref = next(it) if has_state else None
    y_ref = next(it)
    sout_ref = next(it) if want_state else None
    qn_ref, kn_ref, t_ref, qk_ref, gc_ref, gr_ref, o_ref, s_ref = (next(it) for _ in range(8))
    v_refs = (v0_ref, v1_ref)
    z_refs = (z0_ref, z1_ref)
    dk = q_ref.shape[1]
    dv = v0_ref.shape[1]

    qv = q_ref[...]
    qn_ref[...] = (qv * lax.rsqrt(jnp.sum(qv * qv, -1, keepdims=True) + RMS_EPS) * (dk ** -0.5)).astype(BF16)
    kv = k_ref[...]
    kn_ref[...] = (kv * lax.rsqrt(jnp.sum(kv * kv, -1, keepdims=True) + RMS_EPS)).astype(BF16)

    ii = lax.broadcasted_iota(jnp.int32, (cl, cl), 0)
    jj = lax.broadcasted_iota(jnp.int32, (cl, cl), 1)
    eye = (ii == jj).astype(F32)
    tri_f = (ii >= jj).astype(F32)
    incl = (ii >= jj, ii <= jj)
    strict = (ii > jj, ii < jj)
    lane = lax.broadcasted_iota(jnp.int32, (cl, LANES), 1)
    sub = lax.broadcasted_iota(jnp.int32, (SUBLANES, cl), 0)
    n_lvl = int(math.log2(cl))
    pair_mask = [((ii >> (k + 1)) == (jj >> (k + 1))) & ((ii >> k) != (jj >> k)) for k in range(n_lvl)]
    nt = (((1,), (1,)), ((), ()))

    def phase_a(t, carry):
        rows = pl.ds(pl.multiple_of(t * cl, cl), cl)
        kc = kn_ref[rows, :]
        qc = qn_ref[rows, :]
        kk = lax.dot_general(kc, kc, nt, preferred_element_type=F32)
        qk = lax.dot_general(qc, kc, nt, preferred_element_type=F32)
        gcol = gcol_ref[rows, :]
        pre = jnp.dot(tri_f, gcol, precision=HI, preferred_element_type=F32)
        suf = jnp.sum(gcol, axis=0, keepdims=True) - pre + gcol
        gcb = jnp.where(lane >= 4, gcol, jnp.where(lane >= 2, suf, pre))
        gc_ref[rows, :] = gcb
        grow = grow_ref[t]
        pre_r = lax.dot_general(grow, tri_f, nt, precision=HI, preferred_element_type=F32)
        suf_r = jnp.sum(grow, axis=1, keepdims=True) - pre_r + grow
        grb = jnp.where(sub >= 4, grow, jnp.where(sub >= 2, suf_r, pre_r))
        gr_ref[t] = grb
        for dd in range(2):
            for e in range(2):
                kx = dd * 2 + e
                gi = gcb[:, kx:kx + 1]
                gj = grb[kx:kx + 1, :]
                bi = gcb[:, 4 + kx:5 + kx]
                dec = jnp.where(incl[dd], jnp.exp(gi - gj), 0.0)
                nm = jnp.where(strict[dd], bi * kk * dec, 0.0)
                tm = eye - jnp.where(pair_mask[0], nm, 0.0)
                for lvl in range(1, n_lvl):
                    tb = tm.astype(BF16)
                    ct = jnp.dot(jnp.where(pair_mask[lvl], nm, 0.0).astype(BF16), tb, preferred_element_type=F32)
                    tm = tm - jnp.dot(tb, ct.astype(BF16), preferred_element_type=F32)
                t_ref[t, kx] = tm.astype(BF16)
                qk_ref[t, kx] = (qk * dec).astype(BF16)
        return carry

    lax.fori_loop(0, nc, phase_a, 0)

    for kx in range(4):
        if has_state:
            s_ref[kx] = s0_ref[kx // 2, kx % 2]
        else:
            s_ref[kx] = jnp.zeros((dk, dv), F32)

    def phase_b(t, carry):
        for dd in range(2):
            cidx = t if dd == 0 else nc - 1 - t
            rows = pl.ds(pl.multiple_of(cidx * cl, cl), cl)
            kc = kn_ref[rows, :]
            qc = qn_ref[rows, :]
            gcb = gc_ref[rows, :]
            for e in range(2):
                kx = dd * 2 + e
                s = s_ref[kx]
                sb = s.astype(BF16)
                gi = gcb[:, kx:kx + 1]
                bi = gcb[:, 4 + kx:5 + kx]
                egi = jnp.exp(gi)
                ks = jnp.dot(kc, sb, preferred_element_type=F32)
                qs = jnp.dot(qc, sb, preferred_element_type=F32)
                rhs = bi * (v_refs[e][rows, :] - egi * ks)
                vnew = jnp.dot(t_ref[cidx, kx], rhs.astype(BF16), preferred_element_type=F32)
                vb = vnew.astype(BF16)
                o_ref[dd, e, rows, :] = egi * qs + jnp.dot(qk_ref[cidx, kx], vb, preferred_element_type=F32)
                glast = gi[cl - 1:cl, :] if dd == 0 else gi[0:1, :]
                vs = (jnp.exp(glast - gi) * vnew).astype(BF16)
                s_ref[kx] = s * jnp.exp(glast) + lax.dot_general(kc, vs, (((0,), (0,)), ((), ())),
                                                                 preferred_element_type=F32)
        return carry

    lax.fori_loop(0, nc, phase_b, 0)

    for e in range(2):
        o = o_ref[0, e] + o_ref[1, e]
        on = o * lax.rsqrt(jnp.mean(o * o, -1, keepdims=True) + RMS_EPS)
        y_ref[:, e * dv:(e + 1) * dv] = (on * nw_ref[...] * _silu(z_refs[e][...])).astype(y_ref.dtype)
    if want_state:
        for kx in range(4):
            sout_ref[kx // 2, kx % 2] = s_ref[kx]


def _gdn_call(dm, act, proj, gcol, grow, nw, y_prev, s0, l, row_off, nseq, length):
    cl = min(SCAN_CHUNK, length)
    nc = length // cl
    assert row_off % length == 0
    base = row_off // length
    dk, dv = GDN_DK, GDN_DV
    has_state = s0 is not None
    want_state = not has_state
    kcol = dm.qkh
    vcol = 2 * dm.qkh
    zcol = dm.off_zg // dv
    assert dm.off_zg % dv == 0

    in_specs = [pl.BlockSpec((length, dk), lambda b, j: (base + b, j)),
                pl.BlockSpec((length, dk), lambda b, j: (base + b, kcol + j)),
                pl.BlockSpec((length, dv), lambda b, j: (base + b, vcol + 2 * j)),
                pl.BlockSpec((length, dv), lambda b, j: (base + b, vcol + 2 * j + 1)),
                pl.BlockSpec((length, dv), lambda b, j: (base + b, zcol + 2 * j)),
                pl.BlockSpec((length, dv), lambda b, j: (base + b, zcol + 2 * j + 1)),
                pl.BlockSpec((None, length, LANES), lambda b, j: (j, base + b, 0)),
                pl.BlockSpec((None, nc, SUBLANES, cl), lambda b, j: (j, b, 0, 0)),
                pl.BlockSpec((1, dv), lambda b, j: (0, 0))]
    args = [act, act, act, act, proj, proj, gcol, grow, nw]
    if has_state:
        in_specs.append(pl.BlockSpec((None, None, 2, 2, dk, dv), lambda b, j: (b, l, 0, j, 0, 0)))
        args.append(s0)
    out_specs = [pl.BlockSpec((length, 2 * dv), lambda b, j: (base + b, j))]
    out_shape = [jax.ShapeDtypeStruct((dm.t, dm.val), BF16)]
    if want_state:
        out_specs.append(pl.BlockSpec((None, 2, 2, dk, dv), lambda b, j: (b, 0, j, 0, 0)))
        out_shape.append(jax.ShapeDtypeStruct((nseq, 2, dm.vh, dk, dv), F32))
    aliases = {}
    if y_prev is not None:
        in_specs.append(pl.BlockSpec(memory_space=pl.ANY))
        args.append(y_prev)
        aliases = {len(args) - 1: 0}

    def body(*refs):
        refs = list(refs)
        if y_prev is not None:
            del refs[len(args) - 1]
        _gdn_kernel(*refs, nc=nc, cl=cl, has_state=has_state, want_state=want_state)

    return pl.pallas_call(
        body,
        grid=(nseq, dm.qkh),
        in_specs=in_specs,
        out_specs=out_specs,
        out_shape=out_shape,
        scratch_shapes=[pltpu.VMEM((length, dk), BF16), pltpu.VMEM((length, dk), BF16),
                        pltpu.VMEM((nc, 4, cl, cl), BF16), pltpu.VMEM((nc, 4, cl, cl), BF16),
                        pltpu.VMEM((length, LANES), F32), pltpu.VMEM((nc, SUBLANES, cl), F32),
                        pltpu.VMEM((2, 2, length, dv), F32), pltpu.VMEM((4, dk, dv), F32)],
        input_output_aliases=aliases,
        compiler_params=_params(("arbitrary", "arbitrary")),
        name="gdn_scan",
    )(*args)


def _layer_norm(x, g, b):
    mu = jnp.mean(x, -1, keepdims=True)
    xc = x - mu
    var = jnp.mean(xc * xc, -1, keepdims=True)
    return xc * lax.rsqrt(var + LN_EPS) * g + b


def _ln_router_kernel(pre_ref, g_ref, b_ref, sh_ref, sc_ref, rw_ref, x_ref, h_ref, lg_ref):
    x = _layer_norm(pre_ref[...], g_ref[...], b_ref[...])
    x_ref[...] = x
    h = x * (1.0 + sc_ref[...]) + sh_ref[...]
    h_ref[...] = h.astype(BF16)
    lg_ref[...] = jnp.dot(h, rw_ref[...], precision=HI, preferred_element_type=F32)


def _ln_router(dm, pre, g, b, mod4, rw_pad, l):
    tm = _tile(256, dm.seq, dm.dseq)
    row = lambda i: (i, 0)
    fix = lambda i: (0, 0)
    ne = rw_pad.shape[1]
    return pl.pallas_call(
        _ln_router_kernel,
        grid=(dm.t // tm,),
        in_specs=[pl.BlockSpec((tm, dm.d), row),
                  pl.BlockSpec((None, 1, dm.d), lambda i: (l, 0, 0)),
                  pl.BlockSpec((None, 1, dm.d), lambda i: (l, 0, 0)),
                  _mod_spec(dm, tm, l, 3), _mod_spec(dm, tm, l, 4),
                  pl.BlockSpec((dm.d, ne), fix)],
        out_specs=[pl.BlockSpec((tm, dm.d), row), pl.BlockSpec((tm, dm.d), row), pl.BlockSpec((tm, ne), row)],
        out_shape=[jax.ShapeDtypeStruct((dm.t, dm.d), F32), jax.ShapeDtypeStruct((dm.t, dm.d), BF16),
                   jax.ShapeDtypeStruct((dm.t, ne), F32)],
        compiler_params=_params(("arbitrary",)),
        name="ln1_router",
    )(pre, g.reshape(dm.depth, 1, dm.d), b.reshape(dm.depth, 1, dm.d), mod4, mod4, rw_pad)


def _ln_res_kernel(*refs, alpha, with_mod):
    if with_mod:
        x_ref, y_ref, gt_ref, g_ref, b_ref, sh_ref, sc_ref, o_ref, h_ref = refs
    else:
        x_ref, y_ref, gt_ref, g_ref, b_ref, o_ref = refs
    x = _layer_norm(alpha * x_ref[...] + gt_ref[...] * y_ref[...], g_ref[...], b_ref[...])
    o_ref[...] = x
    if with_mod:
        h_ref[...] = (x * (1.0 + sc_ref[...]) + sh_ref[...]).astype(BF16)


def _ln_res(dm, x, y, g, b, mod4, l, alpha):
    tm = _tile(256, dm.seq, dm.dseq)
    row = lambda i: (i, 0)
    with_mod = l + 1 < dm.depth
    in_specs = [pl.BlockSpec((tm, dm.d), row), pl.BlockSpec((tm, dm.d), row), _mod_spec(dm, tm, l, 5),
                pl.BlockSpec((None, 1, dm.d), lambda i: (l, 0, 0)),
                pl.BlockSpec((None, 1, dm.d), lambda i: (l, 0, 0))]
    args = [x, y, mod4, g.reshape(dm.depth, 1, dm.d), b.reshape(dm.depth, 1, dm.d)]
    out_specs = [pl.BlockSpec((tm, dm.d), row)]
    out_shape = [jax.ShapeDtypeStruct((dm.t, dm.d), F32)]
    if with_mod:
        in_specs += [_mod_spec(dm, tm, l + 1, 0), _mod_spec(dm, tm, l + 1, 1)]
        args += [mod4, mod4]
        out_specs.append(pl.BlockSpec((tm, dm.d), row))
        out_shape.append(jax.ShapeDtypeStruct((dm.t, dm.d), BF16))
    return pl.pallas_call(
        functools.partial(_ln_res_kernel, alpha=alpha, with_mod=with_mod),
        grid=(dm.t // tm,),
        in_specs=in_specs,
        out_specs=out_specs,
        out_shape=out_shape,
        compiler_params=_params(("arbitrary",)),
        name="ln2",
    )(*args)


def _ffn_up_kernel(e_ref, nu_ref, x_ref, wg_ref, wu_ref, h_ref, wgb_ref, wub_ref):
    i = pl.program_id(1)
    prev = e_ref[jnp.maximum(i - 1, 0)]

    @pl.when((i < nu_ref[0]) & ((i == 0) | (e_ref[i] != prev)))
    def _():
        wgb_ref[...] = wg_ref[...].astype(BF16)
        wub_ref[...] = wu_ref[...].astype(BF16)

    @pl.when(i < nu_ref[0])
    def _():
        x = x_ref[...]
        a = jnp.dot(x, wgb_ref[...], preferred_element_type=F32)
        u = jnp.dot(x, wub_ref[...], preferred_element_type=F32)
        h_ref[...] = (_silu(a) * u).astype(h_ref.dtype)


def _ffn_down_kernel(e_ref, nu_ref, h_ref, wd_ref, y_ref, wdb_ref):
    i = pl.program_id(1)
    prev = e_ref[jnp.maximum(i - 1, 0)]

    @pl.when((i < nu_ref[0]) & ((i == 0) | (e_ref[i] != prev)))
    def _():
        wdb_ref[...] = wd_ref[...].astype(BF16)

    @pl.when(i < nu_ref[0])
    def _():
        y_ref[...] = jnp.dot(h_ref[...], wdb_ref[...], preferred_element_type=F32)


def _expert_ffn(dm, xb, blk_e, n_used, w_gate, w_up, w_down, l):
    n_buf = xb.shape[0]
    n_blk = n_buf // MOE_ROWS
    tf = _tile(512, dm.dff)
    tn = _tile(2048, dm.d)

    def blk(i, nu):
        return jnp.minimum(i, nu[0] - 1)

    hid = pl.pallas_call(
        _ffn_up_kernel,
        grid_spec=pltpu.PrefetchScalarGridSpec(
            num_scalar_prefetch=2,
            grid=(dm.dff // tf, n_blk),
            in_specs=[pl.BlockSpec((MOE_ROWS, dm.d), lambda f, i, e, nu: (blk(i, nu), 0)),
                      pl.BlockSpec((None, None, dm.d, tf), lambda f, i, e, nu: (l, e[i], 0, f)),
                      pl.BlockSpec((None, None, dm.d, tf), lambda f, i, e, nu: (l, e[i], 0, f))],
            out_specs=pl.BlockSpec((MOE_ROWS, tf), lambda f, i, e, nu: (blk(i, nu), f)),
            scratch_shapes=[pltpu.VMEM((dm.d, tf), BF16), pltpu.VMEM((dm.d, tf), BF16)]),
        out_shape=jax.ShapeDtypeStruct((n_buf, dm.dff), BF16),
        compiler_params=_params(("arbitrary", "arbitrary")),
        name="ffn_up",
    )(blk_e, n_used, xb, w_gate, w_up)

    return pl.pallas_call(
        _ffn_down_kernel,
        grid_spec=pltpu.PrefetchScalarGridSpec(
            num_scalar_prefetch=2,
            grid=(dm.d // tn, n_blk),
            in_specs=[pl.BlockSpec((MOE_ROWS, dm.dff), lambda j, i, e, nu: (blk(i, nu), 0)),
                      pl.BlockSpec((None, None, dm.dff, tn), lambda j, i, e, nu: (l, e[i], 0, j))],
            out_specs=pl.BlockSpec((MOE_ROWS, tn), lambda j, i, e, nu: (blk(i, nu), j)),
            scratch_shapes=[pltpu.VMEM((dm.dff, tn), BF16)]),
        out_shape=jax.ShapeDtypeStruct((n_buf, dm.d), F32),
        compiler_params=_params(("arbitrary", "arbitrary")),
        name="ffn_down",
    )(blk_e, n_used, hid, w_down)


def _route(dm, logits, router_b):
    n_tok = dm.t
    scores = jax.nn.sigmoid(logits[:, :dm.n_exp])
    biased = scores + router_b.astype(F32)
    gsz = dm.n_exp // ROUTER_GROUPS
    group_score = jnp.sum(lax.top_k(biased.reshape(n_tok, ROUTER_GROUPS, gsz), 2)[0], axis=-1)
    sel_group = jnp.argmax(group_score, axis=-1)
    in_group = (jnp.arange(dm.n_exp) // gsz)[None, :] == sel_group[:, None]
    _, top_idx = lax.top_k(jnp.where(in_group, biased, -jnp.inf), TOP_K)
    top_w = jnp.take_along_axis(scores, top_idx, axis=-1)
    top_w = top_w / jnp.sum(top_w, -1, keepdims=True)
    n_slot = n_tok * TOP_K
    flat_e = top_idx.reshape(n_slot).astype(jnp.int32)
    order = jnp.argsort(flat_e).astype(jnp.int32)
    sorted_e = flat_e[order]
    counts = jnp.bincount(flat_e, length=dm.n_exp).astype(jnp.int32)
    padded = (counts + MOE_ROWS - 1) // MOE_ROWS * MOE_ROWS
    start = jnp.cumsum(counts) - counts
    pend = jnp.cumsum(padded)
    pstart = pend - padded
    dest = (pstart[sorted_e] + jnp.arange(n_slot, dtype=jnp.int32) - start[sorted_e]).astype(jnp.int32)
    n_buf = -(-n_slot // MOE_ROWS) * MOE_ROWS + dm.n_exp * MOE_ROWS
    n_blk = n_buf // MOE_ROWS
    buf_tok = jnp.full((n_buf,), n_tok, dtype=jnp.int32).at[dest].set(order // TOP_K)
    blk_e = jnp.minimum(jnp.searchsorted(pend, jnp.arange(n_blk, dtype=jnp.int32) * MOE_ROWS, side='right'),
                        dm.n_exp - 1).astype(jnp.int32)
    slot_dest = jnp.zeros((n_slot,), jnp.int32).at[order].set(dest)
    n_used = (pend[-1] // MOE_ROWS).astype(jnp.int32).reshape(1)
    return top_w, buf_tok, blk_e, slot_dest.reshape(n_tok, TOP_K), n_used


def _moe(dm, h_bf, logits, router_b, w_gate, w_up, w_down, l):
    top_w, buf_tok, blk_e, slot_dest, n_used = _route(dm, logits, router_b)
    x_pad = jnp.concatenate([h_bf, jnp.zeros((1, dm.d), h_bf.dtype)], axis=0)
    xb = x_pad[buf_tok]
    yb = _expert_ffn(dm, xb, blk_e, n_used, w_gate, w_up, w_down, l)
    return jnp.sum(yb[slot_dest] * top_w[..., None], axis=1)


def _grid_pos_embed(rows, d):
    nf = d // 4
    omega = 1.0 / (POS_BASE ** (jnp.arange(nf, dtype=F32) / nf))
    ang_r = jnp.arange(rows, dtype=F32)[:, None] * omega
    ang_c = jnp.arange(GRID_W, dtype=F32)[:, None] * omega
    emb_r = jnp.concatenate([jnp.sin(ang_r), jnp.cos(ang_r)], -1)
    emb_c = jnp.concatenate([jnp.sin(ang_c), jnp.cos(ang_c)], -1)
    emb = jnp.concatenate([jnp.broadcast_to(emb_r[:, None], (rows, GRID_W, d // 2)),
                           jnp.broadcast_to(emb_c[None], (rows, GRID_W, d // 2))], -1)
    return emb.reshape(rows * GRID_W, d)


def _ssd_gate_layouts(dm, dt, a_log):
    g_n, hpg = SSM_GROUPS, dm.hpg
    dt4 = dt.reshape(dm.t, 2, g_n, hpg)
    dtc = jnp.pad(dt4.transpose(1, 2, 0, 3), ((0, 0), (0, 0), (0, 0), (0, LANES - hpg)))
    dtr = dt4.transpose(1, 2, 3, 0)
    al = a_log.astype(F32).reshape(2, g_n, hpg)
    alc = jnp.pad(al[:, :, None, :], ((0, 0), (0, 0), (0, 0), (0, LANES - hpg)))
    alr = al[:, :, :, None]
    return dtc, dtr, alc, alr


def _gdn_gate_layouts(dm, g, beta, cl_list):
    j_n = dm.qkh
    g4 = g.reshape(dm.t, 2, j_n, 2).transpose(2, 0, 1, 3).reshape(j_n, dm.t, 4)
    b4 = beta.reshape(dm.t, 2, j_n, 2).transpose(2, 0, 1, 3).reshape(j_n, dm.t, 4)
    gb = jnp.concatenate([g4, b4], axis=-1)
    gcol = jnp.pad(gb, ((0, 0), (0, 0), (0, LANES - 8)))
    grows = []
    for (row_off, n_rows, cl) in cl_list:
        part = gb[:, row_off:row_off + n_rows]
        grows.append(part.reshape(j_n, n_rows // cl, cl, 8).transpose(0, 1, 3, 2))
    return gcol, grows


def kernel(x_prompt, x_sample, state_ssm, state_gdn, c, c_ctx, w_ada, b_ada, w_in, conv_ssm_w, conv_ssm_b, ssm_dt_bias, ssm_a_log, ssm_d, ssm_norm_w, conv_gdn_w, gdn_dt_bias, gdn_a_log, gdn_norm_w, w_branch_ssm, w_branch_gdn, w_out, ln1_g, ln1_b, ln2_g, ln2_b, router_w, router_b, w_gate, w_up, w_down):
    dm = _dims(x_prompt, x_sample, state_ssm, w_in, w_gate)
    alpha = (2.0 * dm.depth) ** 0.25

    cond = jnp.zeros((dm.ncond, dm.d), F32).at[0].set(c_ctx).at[1:1 + dm.db].set(c)
    mod4 = _ada_mod(dm, cond, w_ada, b_ada).reshape(dm.depth, dm.ncond, 1, 6 * dm.d)
    pos = _grid_pos_embed(dm.dseq // GRID_W, dm.d)
    x, h = _embed(dm, x_prompt.reshape(dm.tp, dm.d), x_sample.reshape(dm.t - dm.tp, dm.d), pos, mod4)

    rw_pad = jnp.pad(router_w, ((0, 0), (0, -router_w.shape[1] % LANES)))
    h0_ssm = state_ssm.reshape(dm.db, dm.depth, 2, SSM_GROUPS, dm.gw, dm.state)
    cl_ctx, cl_lat = min(SCAN_CHUNK, dm.seq), min(SCAN_CHUNK, dm.dseq)
    ssm_states, gdn_states = [], []
    for l in range(dm.depth):
        proj = _in_proj(dm, h, w_in, l)
        dt, g, beta = _gates(dm, proj[:, dm.off_dt:dm.off_dt + 2 * dm.heads],
                             proj[:, dm.off_a:dm.off_a + 2 * dm.vh], proj[:, dm.off_b:dm.off_b + 2 * dm.vh],
                             ssm_dt_bias[l], gdn_dt_bias[l], gdn_a_log[l])
        act_s = _conv_act(dm, proj, dm.off_xbc, dm.xbc, conv_ssm_w, conv_ssm_b, l)
        dtc, dtr, alc, alr = _ssd_gate_layouts(dm, dt, ssm_a_log[l])
        dskip = jnp.repeat(ssm_d[l].astype(F32), SSM_HEADDIM).reshape(1, dm.inner)
        nw_s = ssm_norm_w[l].reshape(1, dm.inner)
        y_ssm, hs = _ssd_call(dm, act_s, proj, dtc, dtr, alc, alr, dskip, nw_s, None, None, l, 0, dm.nb, dm.seq)
        (y_ssm,) = _ssd_call(dm, act_s, proj, dtc, dtr, alc, alr, dskip, nw_s, y_ssm, h0_ssm, l, dm.tp, dm.db,
                             dm.dseq)
        ssm_states.append(hs.reshape(dm.nb, 2, dm.heads, SSM_HEADDIM, dm.state))
        act_g = _conv_act(dm, proj, dm.off_qkv, dm.qkv, conv_gdn_w, None, l)
        gcol, (grow_ctx, grow_lat) = _gdn_gate_layouts(
            dm, g, beta, [(0, dm.tp, cl_ctx), (dm.tp, dm.t - dm.tp, cl_lat)])
        nw_g = gdn_norm_w[l].reshape(1, GDN_DV)
        y_gdn, ss = _gdn_call(dm, act_g, proj, gcol, grow_ctx, nw_g, None, None, l, 0, dm.nb, dm.seq)
        (y_gdn,) = _gdn_call(dm, act_g, proj, gcol, grow_lat, nw_g, y_gdn, state_gdn, l, dm.tp, dm.db, dm.dseq)
        gdn_states.append(ss)
        merged = _branch_merge(dm, y_ssm, y_gdn, w_branch_ssm, w_branch_gdn, proj, l)
        pre = _out_proj(dm, merged, w_out, x, mod4, l, alpha)
        x, h2, logits = _ln_router(dm, pre, ln1_g, ln1_b, mod4, rw_pad, l)
        y_moe = _moe(dm, h2, logits, router_b, w_gate, w_up, w_down, l)
        outs = _ln_res(dm, x, y_moe, ln2_g, ln2_b, mod4, l, alpha)
        x = outs[0]
        if l + 1 < dm.depth:
            h = outs[1]

    y_prompt = x[:dm.tp].reshape(dm.nb, dm.seq, dm.d)
    y_sample = x[dm.tp:].reshape(dm.db, dm.dseq, dm.d)
    return (y_prompt, y_sample, jnp.stack(ssm_states, axis=1), jnp.stack(gdn_states, axis=1))
```

```python
import functools
import math
from typing import NamedTuple

import jax
import jax.numpy as jnp
from jax import lax
from jax.experimental import pallas as pl
from jax.experimental.pallas import tpu as pltpu

F32 = jnp.float32
BF16 = jnp.bfloat16
HI = lax.Precision.HIGHEST

CONV_K = 5
SSM_HEADDIM = 64
SSM_GROUPS = 8
GDN_DK = 128
GDN_DV = 128
ROUTER_GROUPS = 8
TOP_K = 2
GRID_W = 64
POS_BASE = 10000.0
LN_EPS = 1e-5
RMS_EPS = 1e-6

LANES = 128
SUBLANES = 8
VMEM_LIMIT = 56 * 1024 * 1024

SCAN_CHUNK = 128
MOE_ROWS = 128


class Dims(NamedTuple):
    d: int
    depth: int
    nb: int
    seq: int
    db: int
    dseq: int
    tp: int
    t: int
    inner: int
    heads: int
    hpg: int
    gw: int
    state: int
    xbc: int
    qkh: int
    vh: int
    qkv: int
    val: int
    n_exp: int
    dff: int
    in_cols: int
    off_xbc: int
    off_dt: int
    off_qkv: int
    off_zg: int
    off_a: int
    off_b: int
    off_ga: int
    off_gb: int
    ncond: int


def _dims(x_prompt, x_sample, state_ssm, w_in, w_gate):
    nb, seq, d = x_prompt.shape
    db, dseq, _ = x_sample.shape
    depth = w_in.shape[0]
    inner = 2 * d
    heads = inner // SSM_HEADDIM
    state = state_ssm.shape[-1]
    xbc = inner + 2 * SSM_GROUPS * state
    qkh = d // GDN_DK
    vh = 2 * qkh
    key_dim = qkh * GDN_DK
    val = vh * GDN_DV
    qkv = 2 * key_dim + val
    off_xbc = inner
    off_dt = off_xbc + xbc
    off_qkv = off_dt + 2 * heads
    off_zg = off_qkv + qkv
    off_a = off_zg + val
    off_b = off_a + 2 * vh
    off_ga = off_b + 2 * vh
    off_gb = off_ga + d
    in_cols = off_gb + d
    assert in_cols == w_in.shape[2]
    ncond = -(-(1 + db) // SUBLANES) * SUBLANES
    return Dims(d=d, depth=depth, nb=nb, seq=seq, db=db, dseq=dseq, tp=nb * seq, t=nb * seq + db * dseq,
                inner=inner, heads=heads, hpg=heads // SSM_GROUPS, gw=inner // SSM_GROUPS, state=state, xbc=xbc,
                qkh=qkh, vh=vh, qkv=qkv, val=val, n_exp=w_gate.shape[1], dff=w_gate.shape[3], in_cols=in_cols,
                off_xbc=off_xbc, off_dt=off_dt, off_qkv=off_qkv, off_zg=off_zg, off_a=off_a, off_b=off_b,
                off_ga=off_ga, off_gb=off_gb, ncond=ncond)


def _tile(pref, *sizes):
    g = 0
    for s in sizes:
        g = math.gcd(g, s)
    t = math.gcd(pref, g)
    return t


def _params(sem, vmem=VMEM_LIMIT):
    return pltpu.CompilerParams(dimension_semantics=sem, vmem_limit_bytes=vmem)


def _sigmoid(x):
    return 1.0 / (1.0 + jnp.exp(-x))


def _silu(x):
    return x * _sigmoid(x)


def _softplus(x):
    return jnp.maximum(x, 0.0) + jnp.log(1.0 + jnp.exp(-jnp.abs(x)))


def _cond_of_row(dm, row0):
    return jnp.where(row0 < dm.tp, 0, 1 + (row0 - dm.tp) // dm.dseq)


def _ada_kernel(c_ref, w_ref, b_ref, o_ref):
    s = _silu(c_ref[...]).astype(BF16)
    o_ref[...] = jnp.dot(s, w_ref[...].astype(BF16), preferred_element_type=F32) + b_ref[...]


def _ada_mod(dm, cond, w_ada, b_ada):
    n = w_ada.shape[2]
    tn = _tile(512, n)
    return pl.pallas_call(
        _ada_kernel,
        grid=(dm.depth, n // tn),
        in_specs=[pl.BlockSpec((dm.ncond, dm.d), lambda l, j: (0, 0)),
                  pl.BlockSpec((None, dm.d, tn), lambda l, j: (l, 0, j)),
                  pl.BlockSpec((None, 1, tn), lambda l, j: (l, 0, j))],
        out_specs=pl.BlockSpec((None, dm.ncond, tn), lambda l, j: (l, 0, j)),
        out_shape=jax.ShapeDtypeStruct((dm.depth, dm.ncond, n), F32),
        compiler_params=_params(("arbitrary", "arbitrary")),
        name="ada_mod",
    )(cond, w_ada, b_ada.reshape(dm.depth, 1, n))


def _mod_spec(dm, tm, l, which, width=None, col_of=None):
    width = dm.d if width is None else width
    per = dm.d // width

    def idx(*g):
        i, j = (g[0], 0) if col_of is None else col_of(*g)
        return (l, _cond_of_row(dm, i * tm), 0, which * per + j)

    return pl.BlockSpec((None, None, 1, width), idx)


def _embed_kernel(xp_ref, xs_ref, pos_ref, sh_ref, sc_ref, x_ref, h_ref, *, npb):
    is_ctx = pl.program_id(0) < npb
    x = jnp.where(is_ctx, xp_ref[...], xs_ref[...] + pos_ref[...])
    x_ref[...] = x
    h_ref[...] = (x * (1.0 + sc_ref[...]) + sh_ref[...]).astype(BF16)


def _embed(dm, xp, xs, pos, mod4):
    tm = _tile(256, dm.seq, dm.dseq)
    npb = dm.tp // tm
    nsb = (dm.t - dm.tp) // tm
    ppb = dm.dseq // tm
    row = lambda i: (i, 0)
    return pl.pallas_call(
        functools.partial(_embed_kernel, npb=npb),
        grid=(dm.t // tm,),
        in_specs=[pl.BlockSpec((tm, dm.d), lambda i: (jnp.minimum(i, npb - 1), 0)),
                  pl.BlockSpec((tm, dm.d), lambda i: (jnp.clip(i - npb, 0, nsb - 1), 0)),
                  pl.BlockSpec((tm, dm.d), lambda i: (jnp.maximum(i - npb, 0) % ppb, 0)),
                  _mod_spec(dm, tm, 0, 0), _mod_spec(dm, tm, 0, 1)],
        out_specs=[pl.BlockSpec((tm, dm.d), row), pl.BlockSpec((tm, dm.d), row)],
        out_shape=[jax.ShapeDtypeStruct((dm.t, dm.d), F32), jax.ShapeDtypeStruct((dm.t, dm.d), BF16)],
        compiler_params=_params(("arbitrary",)),
        name="embed_mod",
    )(xp, xs, pos, mod4, mod4)


def _mm_res_kernel(x_ref, w_ref, r_ref, g_ref, o_ref, wb_ref, *, alpha):
    @pl.when(pl.program_id(1) == 0)
    def _():
        wb_ref[...] = w_ref[...].astype(BF16)

    acc = jnp.dot(x_ref[...], wb_ref[...], preferred_element_type=F32)
    o_ref[...] = alpha * r_ref[...] + g_ref[...] * acc


def _mm_stream_kernel(x_ref, w_ref, o_ref):
    o_ref[...] = jnp.dot(x_ref[...], w_ref[...].astype(BF16), preferred_element_type=F32).astype(o_ref.dtype)


def _in_proj(dm, h, w_in, l):
    k = dm.d
    n = dm.in_cols
    tm = _tile(2048, dm.t)
    tn = _tile(256, n)
    return pl.pallas_call(
        _mm_stream_kernel,
        grid=(dm.t // tm, n // tn),
        in_specs=[pl.BlockSpec((tm, k), lambda i, j: (i, 0)),
                  pl.BlockSpec((None, k, tn), lambda i, j: (l, 0, j))],
        out_specs=pl.BlockSpec((tm, tn), lambda i, j: (i, j)),
        out_shape=jax.ShapeDtypeStruct((dm.t, n), F32),
        compiler_params=_params(("arbitrary", "arbitrary")),
        name="in_proj",
    )(h, w_in)


def _out_proj(dm, merged, w_out, x, mod4, l, alpha):
    k = dm.d
    tm = _tile(1024, dm.seq, dm.dseq)
    tn = _tile(512, dm.d)
    return pl.pallas_call(
        functools.partial(_mm_res_kernel, alpha=alpha),
        grid=(dm.d // tn, dm.t // tm),
        in_specs=[pl.BlockSpec((tm, k), lambda j, i: (i, 0)),
                  pl.BlockSpec((None, k, tn), lambda j, i: (l, 0, j)),
                  pl.BlockSpec((tm, tn), lambda j, i: (i, j)),
                  _mod_spec(dm, tm, l, 2, width=tn, col_of=lambda j, i: (i, j))],
        out_specs=pl.BlockSpec((tm, tn), lambda j, i: (i, j)),
        out_shape=jax.ShapeDtypeStruct((dm.t, dm.d), F32),
        scratch_shapes=[pltpu.VMEM((k, tn), BF16)],
        compiler_params=_params(("arbitrary", "arbitrary")),
        name="out_proj",
    )(merged, w_out, x, mod4)


def _branch_kernel(ya_ref, yb_ref, wa_ref, wb_ref, ga_ref, gb_ref, o_ref, acc_ref, res_ref, wpa_ref, wpb_ref,
                   *, nk, tk):
    i = pl.program_id(1)
    kb = pl.program_id(2)

    @pl.when(kb % nk == 0)
    def _():
        acc_ref[...] = jnp.zeros_like(acc_ref)

    def accumulate(y_ref, w_ref, wp_ref, kk):
        rows = pl.ds(pl.multiple_of(kk * tk, tk), tk)

        @pl.when(i == 0)
        def _():
            wp_ref[rows, :] = w_ref[...].astype(BF16)

        acc_ref[...] += jnp.dot(y_ref[...], wp_ref[rows, :], preferred_element_type=F32)

    @pl.when(kb < nk)
    def _():
        accumulate(ya_ref, wa_ref, wpa_ref, kb)

    @pl.when(kb >= nk)
    def _():
        accumulate(yb_ref, wb_ref, wpb_ref, kb - nk)

    @pl.when(kb == nk - 1)
    def _():
        res_ref[...] = _sigmoid(ga_ref[...]) * acc_ref[...]

    @pl.when(kb == 2 * nk - 1)
    def _():
        o_ref[...] = (res_ref[...] + _sigmoid(gb_ref[...]) * acc_ref[...]).astype(o_ref.dtype)


def _branch_merge(dm, y_ssm, y_gdn, w_a, w_b, proj, l):
    assert dm.inner == dm.val
    kdim = dm.inner
    tm = _tile(1024, dm.seq, dm.dseq)
    tn = _tile(512, dm.d, dm.off_ga, dm.off_gb)
    tk = _tile(1024, kdim)
    nk = kdim // tk
    ca, cb = dm.off_ga // tn, dm.off_gb // tn

    def wa_idx(j, i, kb):
        return (l, jnp.where(i == 0, jnp.minimum(kb, nk - 1), nk - 1), j)

    def wb_idx(j, i, kb):
        return (l, jnp.where(i == 0, jnp.maximum(kb - nk, 0), nk - 1), j)

    return pl.pallas_call(
        functools.partial(_branch_kernel, nk=nk, tk=tk),
        grid=(dm.d // tn, dm.t // tm, 2 * nk),
        in_specs=[pl.BlockSpec((tm, tk), lambda j, i, kb: (i, jnp.minimum(kb, nk - 1))),
                  pl.BlockSpec((tm, tk), lambda j, i, kb: (i, jnp.maximum(kb - nk, 0))),
                  pl.BlockSpec((None, tk, tn), wa_idx),
                  pl.BlockSpec((None, tk, tn), wb_idx),
                  pl.BlockSpec((tm, tn), lambda j, i, kb: (i, ca + j)),
                  pl.BlockSpec((tm, tn), lambda j, i, kb: (i, cb + j))],
        out_specs=pl.BlockSpec((tm, tn), lambda j, i, kb: (i, j)),
        out_shape=jax.ShapeDtypeStruct((dm.t, dm.d), BF16),
        scratch_shapes=[pltpu.VMEM((tm, tn), F32), pltpu.VMEM((tm, tn), F32),
                        pltpu.VMEM((kdim, tn), BF16), pltpu.VMEM((kdim, tn), BF16)],
        compiler_params=_params(("arbitrary", "arbitrary", "arbitrary")),
        name="branch_merge",
    )(y_ssm, y_gdn, w_a, w_b, proj, proj)


def _gates_kernel(dtr_ref, ar_ref, br_ref, dtb_ref, gb_ref, al_ref, dt_ref, g_ref, beta_ref):
    dt_ref[...] = _softplus(dtr_ref[...] + dtb_ref[...])
    g_ref[...] = -jnp.exp(al_ref[...]) * _softplus(ar_ref[...] + gb_ref[...])
    beta_ref[...] = _sigmoid(br_ref[...])


def _gates(dm, dt_raw, a_raw, b_raw, dt_bias, g_bias, a_log):
    tm = _tile(512, dm.t)
    w1, w2 = 2 * dm.heads, 2 * dm.vh
    row = lambda i: (i, 0)
    fix = lambda i: (0, 0)
    return pl.pallas_call(
        _gates_kernel,
        grid=(dm.t // tm,),
        in_specs=[pl.BlockSpec((tm, w1), row), pl.BlockSpec((tm, w2), row), pl.BlockSpec((tm, w2), row),
                  pl.BlockSpec((1, w1), fix), pl.BlockSpec((1, w2), fix), pl.BlockSpec((1, w2), fix)],
        out_specs=[pl.BlockSpec((tm, w1), row), pl.BlockSpec((tm, w2), row), pl.BlockSpec((tm, w2), row)],
        out_shape=[jax.ShapeDtypeStruct((dm.t, w1), F32), jax.ShapeDtypeStruct((dm.t, w2), F32),
                   jax.ShapeDtypeStruct((dm.t, w2), F32)],
        compiler_params=_params(("arbitrary",)),
        name="gate_prep",
    )(dt_raw, a_raw, b_raw, dt_bias.reshape(1, w1), g_bias.reshape(1, w2), a_log.reshape(1, w2))


def _conv_kernel(*refs, rows, blocks_ctx, blocks_per_ctx_seq, blocks_per_lat_seq, has_bias):
    if has_bias:
        x_ref, prev_ref, next_ref, w_ref, b_ref, o_ref = refs
    else:
        x_ref, prev_ref, next_ref, w_ref, o_ref = refs
        b_ref = None
    i = pl.program_id(0)
    in_ctx = i < blocks_ctx
    pos = jnp.where(in_ctx, i % blocks_per_ctx_seq, (i - blocks_ctx) % blocks_per_lat_seq)
    per = jnp.where(in_ctx, blocks_per_ctx_seq, blocks_per_lat_seq)
    prev = jnp.where(pos == 0, 0.0, prev_ref[...])
    nxt = jnp.where(pos == per - 1, 0.0, next_ref[...])
    ext = jnp.concatenate([prev, x_ref[...], nxt], axis=0)
    n = rows + 2 * SUBLANES
    pad = CONV_K // 2
    w = w_ref[...]
    acc = None
    for tap in range(CONV_K):
        start = SUBLANES - pad + tap
        shifted = ext if start == 0 else pltpu.roll(ext, n - start, 0)
        term = shifted[:rows] * w[tap:tap + 1, :]
        acc = term if acc is None else acc + term
    if has_bias:
        acc = acc + b_ref[...]
    o_ref[...] = _silu(acc)


def _conv_act(dm, proj, col_off, width, w, b, l):
    rows = _tile(256, dm.seq, dm.dseq)
    tc = _tile(2048, col_off, width)
    c0 = col_off // tc
    hb = rows // SUBLANES
    nhalo = dm.t // SUBLANES
    has_bias = b is not None
    in_specs = [pl.BlockSpec((rows, tc), lambda i, j: (i, c0 + j)),
                pl.BlockSpec((SUBLANES, tc), lambda i, j: (jnp.maximum(i * hb - 1, 0), c0 + j)),
                pl.BlockSpec((SUBLANES, tc), lambda i, j: (jnp.minimum((i + 1) * hb, nhalo - 1), c0 + j)),
                pl.BlockSpec((None, CONV_K, tc), lambda i, j: (l, 0, j))]
    args = [proj, proj, proj, w]
    if has_bias:
        in_specs.append(pl.BlockSpec((None, 1, tc), lambda i, j: (l, 0, j)))
        args.append(b.reshape(dm.depth, 1, width))
    return pl.pallas_call(
        functools.partial(_conv_kernel, rows=rows, blocks_ctx=dm.tp // rows, blocks_per_ctx_seq=dm.seq // rows,
                          blocks_per_lat_seq=dm.dseq // rows, has_bias=has_bias),
        grid=(dm.t // rows, width // tc),
        in_specs=in_specs,
        out_specs=pl.BlockSpec((rows, tc), lambda i, j: (i, j)),
        out_shape=jax.ShapeDtypeStruct((dm.t, width), F32),
        compiler_params=_params(("arbitrary", "arbitrary")),
        name="conv_act",
    )(*args)


def _ssd_kernel(*refs, nc, hpg, hd, has_state, want_state):
    it = iter(refs)
    xs_ref, b_ref, c_ref, dtc_ref, dtr_ref, alc_ref, alr_ref, z_ref, dsk_ref, nw_ref = (next(it) for _ in range(10))
    h0_ref = next(it) if has_state else None
    y_ref = next(it)
    hout_ref = next(it) if want_state else None
    ht_ref, yf_ref, xw_ref, yc_ref = (next(it) for _ in range(4))

    d = pl.program_id(2)
    c = pl.program_id(3)
    q = xs_ref.shape[0]
    fwd = d == 0
    ii = lax.broadcasted_iota(jnp.int32, (q, q), 0)
    jj = lax.broadcasted_iota(jnp.int32, (q, q), 1)
    mask = ((ii - jj) * (1 - 2 * d)) >= 0
    tri = mask.astype(F32)

    dt_c = dtc_ref[...]
    a_c = dt_c * (-jnp.exp(alc_ref[...]))
    acum_c = jnp.dot(tri, a_c, precision=HI, preferred_element_type=F32)
    dt_r = dtr_ref[...]
    a_r = dt_r * (-jnp.exp(alr_ref[...]))
    acum_r = lax.dot_general(a_r, tri, (((1,), (1,)), ((), ())), precision=HI, preferred_element_type=F32)
    alast_c = jnp.where(fwd, acum_c[q - 1:q, :], acum_c[0:1, :])
    alast_r = jnp.where(fwd, acum_r[:, q - 1:q], acum_r[:, 0:1])

    bm = b_ref[...].astype(BF16)
    cm = c_ref[...].astype(BF16)
    cb = lax.dot_general(cm, bm, (((1,), (1,)), ((), ())), preferred_element_type=F32)

    @pl.when(c == 0)
    def _():
        if has_state:
            ht_ref[...] = h0_ref[...].T
        else:
            ht_ref[...] = jnp.zeros_like(ht_ref)

    ht = ht_ref[...]
    yo = jnp.dot(cm, ht.astype(BF16), preferred_element_type=F32)
    xs = xs_ref[...]
    for r in range(hpg):
        sl = slice(r * hd, (r + 1) * hd)
        ac = acum_c[:, r:r + 1]
        ar = acum_r[r:r + 1, :]
        dec = jnp.where(mask, jnp.exp(ac - ar), 0.0)
        wgt = (cb * dec * dt_r[r:r + 1, :]).astype(BF16)
        x_r = xs[:, sl]
        yd = jnp.dot(wgt, x_r.astype(BF16), preferred_element_type=F32)
        yc_ref[:, sl] = yd + yo[:, sl] * jnp.exp(ac)
        xw_ref[:, sl] = (x_r * (jnp.exp(alast_c[:, r:r + 1] - ac) * dt_c[:, r:r + 1])).astype(BF16)
    st = lax.dot_general(bm, xw_ref[...], (((0,), (0,)), ((), ())), preferred_element_type=F32)
    for r in range(hpg):
        sl = slice(r * hd, (r + 1) * hd)
        ht_ref[:, sl] = ht[:, sl] * jnp.exp(alast_r[r:r + 1, :]) + st[:, sl]

    cidx = c + d * (nc - 1 - 2 * c)
    rows = pl.ds(pl.multiple_of(cidx * q, q), q)

    @pl.when(fwd)
    def _():
        yf_ref[rows, :] = yc_ref[...]

    @pl.when(d == 1)
    def _():
        tot = yf_ref[rows, :] + yc_ref[...] + dsk_ref[...] * xs
        gated = tot * _silu(z_ref[...])
        ms = jnp.mean(gated * gated, axis=-1, keepdims=True)
        y_ref[...] = (gated * lax.rsqrt(ms + RMS_EPS) * nw_ref[...]).astype(y_ref.dtype)

    if want_state:
        @pl.when(c == nc - 1)
        def _():
            hout_ref[...] = ht_ref[...].T


def _ssd_call(dm, act, proj, dtc, dtr, alc, alr, dskip, nw, y_prev, h0, l, row_off, nseq, length):
    q = min(SCAN_CHUNK, length)
    nc = length // q
    base = row_off // q
    g_n, gw, n = SSM_GROUPS, dm.gw, dm.state
    has_state = h0 is not None
    want_state = not has_state
    bcol = dm.inner // n

    def rb(b, d, c):
        return base + b * nc + c + d * (nc - 1 - 2 * c)

    def rb_out(b, d, c):
        return base + b * nc + jnp.where(d == 0, nc - 1, nc - 1 - c)

    in_specs = [pl.BlockSpec((q, gw), lambda b, g, d, c: (rb(b, d, c), g)),
                pl.BlockSpec((q, n), lambda b, g, d, c: (rb(b, d, c), bcol + g)),
                pl.BlockSpec((q, n), lambda b, g, d, c: (rb(b, d, c), bcol + g_n + g)),
                pl.BlockSpec((None, None, q, LANES), lambda b, g, d, c: (d, g, rb(b, d, c), 0)),
                pl.BlockSpec((None, None, dm.hpg, q), lambda b, g, d, c: (d, g, 0, rb(b, d, c))),
                pl.BlockSpec((None, None, 1, LANES), lambda b, g, d, c: (d, g, 0, 0)),
                pl.BlockSpec((None, None, dm.hpg, 1), lambda b, g, d, c: (d, g, 0, 0)),
                pl.BlockSpec((q, gw), lambda b, g, d, c: (rb_out(b, d, c), g)),
                pl.BlockSpec((1, gw), lambda b, g, d, c: (0, g)),
                pl.BlockSpec((1, gw), lambda b, g, d, c: (0, g))]
    args = [act, act, act, dtc, dtr, alc, alr, proj, dskip, nw]
    if has_state:
        in_specs.append(pl.BlockSpec((None, None, None, None, gw, n), lambda b, g, d, c: (b, l, d, g, 0, 0)))
        args.append(h0)
    out_specs = [pl.BlockSpec((q, gw), lambda b, g, d, c: (rb_out(b, d, c), g))]
    out_shape = [jax.ShapeDtypeStruct((dm.t, dm.inner), BF16)]
    if want_state:
        out_specs.append(pl.BlockSpec((None, None, None, gw, n), lambda b, g, d, c: (b, d, g, 0, 0)))
        out_shape.append(jax.ShapeDtypeStruct((nseq, 2, g_n, gw, n), F32))
    aliases = {}
    if y_prev is not None:
        in_specs.append(pl.BlockSpec(memory_space=pl.ANY))
        args.append(y_prev)
        aliases = {len(args) - 1: 0}

    def body(*refs):
        refs = list(refs)
        if y_prev is not None:
            n_in = len(args)
            del refs[n_in - 1]
        _ssd_kernel(*refs, nc=nc, hpg=dm.hpg, hd=SSM_HEADDIM, has_state=has_state, want_state=want_state)

    return pl.pallas_call(
        body,
        grid=(nseq, g_n, 2, nc),
        in_specs=in_specs,
        out_specs=out_specs,
        out_shape=out_shape,
        scratch_shapes=[pltpu.VMEM((n, gw), F32), pltpu.VMEM((length, gw), F32),
                        pltpu.VMEM((q, gw), BF16), pltpu.VMEM((q, gw), F32)],
        input_output_aliases=aliases,
        compiler_params=_params(("arbitrary",) * 4),
        name="ssd_scan",
    )(*args)


def _conv_silu_seq(x, w):
    n = x.shape[0]
    pad = CONV_K // 2
    t = lax.broadcasted_iota(jnp.int32, x.shape, 0)
    acc = x * w[pad:pad + 1, :]
    for tap in range(CONV_K):
        off = tap - pad
        if off == 0:
            continue
        shifted = pltpu.roll(x, (-off) % n, 0)
        inside = (t >= -off) if off < 0 else (t < n - off)
        acc = acc + jnp.where(inside, shifted, 0.0) * w[tap:tap + 1, :]
    return _silu(acc)


def _gdn_kernel(*refs, nc, cl, has_state, want_state):
    it = iter(refs)
    q_ref, k_ref, v0_ref, v1_ref, wq_ref, wk_ref, wv0_ref, wv1_ref = (next(it) for _ in range(8))
    z0_ref, z1_ref, gcol_ref, grow_ref, nw_ref = (next(it) for _ in range(5))
    s0_ref = next(it) if has_state else None
    y_ref = next(it)
    sout_ref = next(it) if want_state else None
    qn_ref, kn_ref, va_ref, t_ref, qk_ref, gc_ref, o_ref, s_ref = (next(it) for _ in range(8))
    z_refs = (z0_ref, z1_ref)
    dk = q_ref.shape[1]
    dv = v0_ref.shape[1]

    qv = _conv_silu_seq(q_ref[...], wq_ref[...])
    qn_ref[...] = (qv * lax.rsqrt(jnp.sum(qv * qv, -1, keepdims=True) + RMS_EPS) * (dk ** -0.5)).astype(BF16)
    kv = _conv_silu_seq(k_ref[...], wk_ref[...])
    kn_ref[...] = (kv * lax.rsqrt(jnp.sum(kv * kv, -1, keepdims=True) + RMS_EPS)).astype(BF16)
    va_ref[0] = _conv_silu_seq(v0_ref[...], wv0_ref[...])
    va_ref[1] = _conv_silu_seq(v1_ref[...], wv1_ref[...])

    ii = lax.broadcasted_iota(jnp.int32, (cl, cl), 0)
    jj = lax.broadcasted_iota(jnp.int32, (cl, cl), 1)
    eye = (ii == jj).astype(F32)
    tri_f = (ii >= jj).astype(F32)
    incl = (ii >= jj, ii <= jj)
    strict = (ii > jj, ii < jj)
    lane = lax.broadcasted_iota(jnp.int32, (cl, LANES), 1)
    sub = lax.broadcasted_iota(jnp.int32, (SUBLANES, cl), 0)
    n_lvl = int(math.log2(cl))
    pair_mask = [((ii >> (k + 1)) == (jj >> (k + 1))) & ((ii >> k) != (jj >> k)) for k in range(n_lvl)]
    nt = (((1,), (1,)), ((), ()))

    ua = 2 if nc % 2 == 0 else 1

    def phase_a(t, carry):
        nms, tms, slots = [], [], []
        for u in range(ua):
            ci = t * ua + u
            rows = pl.ds(pl.multiple_of(ci * cl, cl), cl)
            kc = kn_ref[rows, :]
            qc = qn_ref[rows, :]
            kk = lax.dot_general(kc, kc, nt, preferred_element_type=F32)
            qk = lax.dot_general(qc, kc, nt, preferred_element_type=F32)
            gcol = gcol_ref[rows, :]
            pre = jnp.dot(tri_f, gcol, precision=HI, preferred_element_type=F32)
            suf = jnp.sum(gcol, axis=0, keepdims=True) - pre + gcol
            gcb = jnp.where(lane >= 4, gcol, jnp.where(lane >= 2, suf, pre))
            gc_ref[rows, :] = gcb
            grow = grow_ref[ci]
            pre_r = lax.dot_general(grow, tri_f, nt, precision=HI, preferred_element_type=F32)
            suf_r = jnp.sum(grow, axis=1, keepdims=True) - pre_r + grow
            grb = jnp.where(sub >= 4, grow, jnp.where(sub >= 2, suf_r, pre_r))
            for kx in range(4):
                dd = kx // 2
                gi = gcb[:, kx:kx + 1]
                gj = grb[kx:kx + 1, :]
                bi = gcb[:, 4 + kx:5 + kx]
                dec = jnp.where(incl[dd], jnp.exp(gi - gj), 0.0)
                nm = jnp.where(strict[dd], bi * kk * dec, 0.0)
                qk_ref[ci, kx] = (qk * dec).astype(BF16)
                nms.append(nm)
                tms.append(eye - jnp.where(pair_mask[0], nm, 0.0))
                slots.append((ci, kx))
        for lvl in range(1, n_lvl):
            tbs = [tm.astype(BF16) for tm in tms]
            cts = [jnp.dot(jnp.where(pair_mask[lvl], nm, 0.0).astype(BF16), tb, preferred_element_type=F32)
                   for nm, tb in zip(nms, tbs)]
            tms = [tm - jnp.dot(tb, ct.astype(BF16), preferred_element_type=F32)
                   for tm, tb, ct in zip(tms, tbs, cts)]
        for (ci, kx), tm in zip(slots, tms):
            t_ref[ci, kx] = tm.astype(BF16)
        return carry

    lax.fori_loop(0, nc // ua, phase_a, 0)

    for kx in range(4):
        if has_state:
            s_ref[kx] = s0_ref[kx // 2, kx % 2]
        else:
            s_ref[kx] = jnp.zeros((dk, dv), F32)

    def phase_b(t, carry):
        cidx = (t, t, nc - 1 - t, nc - 1 - t)
        rows = [pl.ds(pl.multiple_of(ci * cl, cl), cl) for ci in cidx]
        kcs = [kn_ref[r, :] for r in rows]
        ss = [s_ref[kx] for kx in range(4)]
        gis = [gc_ref[rows[kx], :][:, kx:kx + 1] for kx in range(4)]
        bis = [gc_ref[rows[kx], :][:, 4 + kx:5 + kx] for kx in range(4)]
        egis = [jnp.exp(gi) for gi in gis]
        kqs = [jnp.dot(jnp.concatenate([kcs[kx], qn_ref[rows[kx], :]], axis=0), ss[kx].astype(BF16),
                       preferred_element_type=F32) for kx in range(4)]
        rhs = [(bis[kx] * (va_ref[kx % 2, rows[kx], :] - egis[kx] * kqs[kx][:cl])).astype(BF16) for kx in range(4)]
        vnews = [jnp.dot(t_ref[cidx[kx], kx], rhs[kx], preferred_element_type=F32) for kx in range(4)]
        for kx in range(4):
            o_ref[kx // 2, kx % 2, rows[kx], :] = egis[kx] * kqs[kx][cl:] + jnp.dot(
                qk_ref[cidx[kx], kx], vnews[kx].astype(BF16), preferred_element_type=F32)
        glasts = [gis[kx][cl - 1:cl, :] if kx < 2 else gis[kx][0:1, :] for kx in range(4)]
        vss = [(jnp.exp(glasts[kx] - gis[kx]) * vnews[kx]).astype(BF16) for kx in range(4)]
        for kx in range(4):
            s_ref[kx] = ss[kx] * jnp.exp(glasts[kx]) + lax.dot_general(
                kcs[kx], vss[kx], (((0,), (0,)), ((), ())), preferred_element_type=F32)
        return carry

    lax.fori_loop(0, nc, phase_b, 0)

    for e in range(2):
        o = o_ref[0, e] + o_ref[1, e]
        on = o * lax.rsqrt(jnp.mean(o * o, -1, keepdims=True) + RMS_EPS)
        y_ref[:, e * dv:(e + 1) * dv] = (on * nw_ref[...] * _silu(z_refs[e][...])).astype(y_ref.dtype)
    if want_state:
        for kx in range(4):
            sout_ref[kx // 2, kx % 2] = s_ref[kx]


def _gdn_call(dm, proj, conv_w, gcol, grow, nw, y_prev, s0, l, row_off, nseq, length):
    cl = min(SCAN_CHUNK, length)
    nc = length // cl
    assert row_off % length == 0
    base = row_off // length
    dk, dv = GDN_DK, GDN_DV
    has_state = s0 is not None
    want_state = not has_state
    assert dm.off_qkv % dk == 0 and dm.off_zg % dv == 0
    qcol = dm.off_qkv // dk
    kcol = dm.qkh
    vcol = 2 * dm.qkh
    zcol = dm.off_zg // dv

    def col_specs(first, shape, row_of):
        return [pl.BlockSpec(shape, lambda b, j: row_of(b) + (first + j,)),
                pl.BlockSpec(shape, lambda b, j: row_of(b) + (first + kcol + j,)),
                pl.BlockSpec(shape, lambda b, j: row_of(b) + (first + vcol + 2 * j,)),
                pl.BlockSpec(shape, lambda b, j: row_of(b) + (first + vcol + 2 * j + 1,))]

    in_specs = (col_specs(qcol, (length, dk), lambda b: (base + b,))
                + col_specs(0, (None, CONV_K, dk), lambda b: (l, 0))
                + [pl.BlockSpec((length, dv), lambda b, j: (base + b, zcol + 2 * j)),
                   pl.BlockSpec((length, dv), lambda b, j: (base + b, zcol + 2 * j + 1)),
                   pl.BlockSpec((None, length, LANES), lambda b, j: (j, base + b, 0)),
                   pl.BlockSpec((None, nc, SUBLANES, cl), lambda b, j: (j, b, 0, 0)),
                   pl.BlockSpec((1, dv), lambda b, j: (0, 0))])
    args = [proj] * 4 + [conv_w] * 4 + [proj, proj, gcol, grow, nw]
    if has_state:
        in_specs.append(pl.BlockSpec((None, None, 2, 2, dk, dv), lambda b, j: (b, l, 0, j, 0, 0)))
        args.append(s0)
    out_specs = [pl.BlockSpec((length, 2 * dv), lambda b, j: (base + b, j))]
    out_shape = [jax.ShapeDtypeStruct((dm.t, dm.val), BF16)]
    if want_state:
        out_specs.append(pl.BlockSpec((None, 2, 2, dk, dv), lambda b, j: (b, 0, j, 0, 0)))
        out_shape.append(jax.ShapeDtypeStruct((nseq, 2, dm.vh, dk, dv), F32))
    aliases = {}
    if y_prev is not None:
        in_specs.append(pl.BlockSpec(memory_space=pl.ANY))
        args.append(y_prev)
        aliases = {len(args) - 1: 0}

    def body(*refs):
        refs = list(refs)
        if y_prev is not None:
            del refs[len(args) - 1]
        _gdn_kernel(*refs, nc=nc, cl=cl, has_state=has_state, want_state=want_state)

    return pl.pallas_call(
        body,
        grid=(nseq, dm.qkh),
        in_specs=in_specs,
        out_specs=out_specs,
        out_shape=out_shape,
        scratch_shapes=[pltpu.VMEM((length, dk), BF16), pltpu.VMEM((length, dk), BF16),
                        pltpu.VMEM((2, length, dv), F32),
                        pltpu.VMEM((nc, 4, cl, cl), BF16), pltpu.VMEM((nc, 4, cl, cl), BF16),
                        pltpu.VMEM((length, LANES), F32),
                        pltpu.VMEM((2, 2, length, dv), F32), pltpu.VMEM((4, dk, dv), F32)],
        input_output_aliases=aliases,
        compiler_params=_params(("arbitrary", "arbitrary")),
        name="gdn_scan",
    )(*args)


def _layer_norm(x, g, b):
    mu = jnp.mean(x, -1, keepdims=True)
    xc = x - mu
    var = jnp.mean(xc * xc, -1, keepdims=True)
    return xc * lax.rsqrt(var + LN_EPS) * g + b


def _ln_router_kernel(pre_ref, g_ref, b_ref, sh_ref, sc_ref, rw_ref, x_ref, h_ref, lg_ref):
    x = _layer_norm(pre_ref[...], g_ref[...], b_ref[...])
    x_ref[...] = x
    h = x * (1.0 + sc_ref[...]) + sh_ref[...]
    h_ref[...] = h.astype(BF16)
    lg_ref[...] = jnp.dot(h, rw_ref[...], precision=HI, preferred_element_type=F32)


def _ln_router(dm, pre, g, b, mod4, rw_pad, l):
    tm = _tile(256, dm.seq, dm.dseq)
    row = lambda i: (i, 0)
    fix = lambda i: (0, 0)
    ne = rw_pad.shape[1]
    return pl.pallas_call(
        _ln_router_kernel,
        grid=(dm.t // tm,),
        in_specs=[pl.BlockSpec((tm, dm.d), row),
                  pl.BlockSpec((None, 1, dm.d), lambda i: (l, 0, 0)),
                  pl.BlockSpec((None, 1, dm.d), lambda i: (l, 0, 0)),
                  _mod_spec(dm, tm, l, 3), _mod_spec(dm, tm, l, 4),
                  pl.BlockSpec((dm.d, ne), fix)],
        out_specs=[pl.BlockSpec((tm, dm.d), row), pl.BlockSpec((tm, dm.d), row), pl.BlockSpec((tm, ne), row)],
        out_shape=[jax.ShapeDtypeStruct((dm.t, dm.d), F32), jax.ShapeDtypeStruct((dm.t, dm.d), BF16),
                   jax.ShapeDtypeStruct((dm.t, ne), F32)],
        compiler_params=_params(("arbitrary",)),
        name="ln1_router",
    )(pre, g.reshape(dm.depth, 1, dm.d), b.reshape(dm.depth, 1, dm.d), mod4, mod4, rw_pad)


def _ln_res_kernel(*refs, alpha, with_mod):
    if with_mod:
        x_ref, y_ref, gt_ref, g_ref, b_ref, sh_ref, sc_ref, o_ref, h_ref = refs
    else:
        x_ref, y_ref, gt_ref, g_ref, b_ref, o_ref = refs
    x = _layer_norm(alpha * x_ref[...] + gt_ref[...] * y_ref[...], g_ref[...], b_ref[...])
    o_ref[...] = x
    if with_mod:
        h_ref[...] = (x * (1.0 + sc_ref[...]) + sh_ref[...]).astype(BF16)


def _ln_res(dm, x, y, g, b, mod4, l, alpha):
    tm = _tile(256, dm.seq, dm.dseq)
    row = lambda i: (i, 0)
    with_mod = l + 1 < dm.depth
    in_specs = [pl.BlockSpec((tm, dm.d), row), pl.BlockSpec((tm, dm.d), row), _mod_spec(dm, tm, l, 5),
                pl.BlockSpec((None, 1, dm.d), lambda i: (l, 0, 0)),
                pl.BlockSpec((None, 1, dm.d), lambda i: (l, 0, 0))]
    args = [x, y, mod4, g.reshape(dm.depth, 1, dm.d), b.reshape(dm.depth, 1, dm.d)]
    out_specs = [pl.BlockSpec((tm, dm.d), row)]
    out_shape = [jax.ShapeDtypeStruct((dm.t, dm.d), F32)]
    if with_mod:
        in_specs += [_mod_spec(dm, tm, l + 1, 0), _mod_spec(dm, tm, l + 1, 1)]
        args += [mod4, mod4]
        out_specs.append(pl.BlockSpec((tm, dm.d), row))
        out_shape.append(jax.ShapeDtypeStruct((dm.t, dm.d), BF16))
    return pl.pallas_call(
        functools.partial(_ln_res_kernel, alpha=alpha, with_mod=with_mod),
        grid=(dm.t // tm,),
        in_specs=in_specs,
        out_specs=out_specs,
        out_shape=out_shape,
        compiler_params=_params(("arbitrary",)),
        name="ln2",
    )(*args)


def _ffn_up_kernel(e_ref, nu_ref, x_ref, wg_ref, wu_ref, h_ref, wgb_ref, wub_ref):
    i = pl.program_id(1)
    prev = e_ref[jnp.maximum(i - 1, 0)]

    @pl.when((i < nu_ref[0]) & ((i == 0) | (e_ref[i] != prev)))
    def _():
        wgb_ref[...] = wg_ref[...].astype(BF16)
        wub_ref[...] = wu_ref[...].astype(BF16)

    @pl.when(i < nu_ref[0])
    def _():
        x = x_ref[...]
        a = jnp.dot(x, wgb_ref[...], preferred_element_type=F32)
        u = jnp.dot(x, wub_ref[...], preferred_element_type=F32)
        h_ref[...] = (_silu(a) * u).astype(h_ref.dtype)


def _ffn_down_kernel(e_ref, nu_ref, h_ref, wd_ref, y_ref, wdb_ref):
    i = pl.program_id(1)
    prev = e_ref[jnp.maximum(i - 1, 0)]

    @pl.when((i < nu_ref[0]) & ((i == 0) | (e_ref[i] != prev)))
    def _():
        wdb_ref[...] = wd_ref[...].astype(BF16)

    @pl.when(i < nu_ref[0])
    def _():
        y_ref[...] = jnp.dot(h_ref[...], wdb_ref[...], preferred_element_type=F32)


def _expert_ffn(dm, xb, blk_e, n_used, w_gate, w_up, w_down, l):
    n_buf = xb.shape[0]
    n_blk = n_buf // MOE_ROWS
    tf = _tile(512, dm.dff)
    tn = _tile(2048, dm.d)

    def blk(i, nu):
        return jnp.minimum(i, nu[0] - 1)

    hid = pl.pallas_call(
        _ffn_up_kernel,
        grid_spec=pltpu.PrefetchScalarGridSpec(
            num_scalar_prefetch=2,
            grid=(dm.dff // tf, n_blk),
            in_specs=[pl.BlockSpec((MOE_ROWS, dm.d), lambda f, i, e, nu: (blk(i, nu), 0)),
                      pl.BlockSpec((None, None, dm.d, tf), lambda f, i, e, nu: (l, e[i], 0, f)),
                      pl.BlockSpec((None, None, dm.d, tf), lambda f, i, e, nu: (l, e[i], 0, f))],
            out_specs=pl.BlockSpec((MOE_ROWS, tf), lambda f, i, e, nu: (blk(i, nu), f)),
            scratch_shapes=[pltpu.VMEM((dm.d, tf), BF16), pltpu.VMEM((dm.d, tf), BF16)]),
        out_shape=jax.ShapeDtypeStruct((n_buf, dm.dff), BF16),
        compiler_params=_params(("arbitrary", "arbitrary")),
        name="ffn_up",
    )(blk_e, n_used, xb, w_gate, w_up)

    return pl.pallas_call(
        _ffn_down_kernel,
        grid_spec=pltpu.PrefetchScalarGridSpec(
            num_scalar_prefetch=2,
            grid=(dm.d // tn, n_blk),
            in_specs=[pl.BlockSpec((MOE_ROWS, dm.dff), lambda j, i, e, nu: (blk(i, nu), 0)),
                      pl.BlockSpec((None, None, dm.dff, tn), lambda j, i, e, nu: (l, e[i], 0, j))],
            out_specs=pl.BlockSpec((MOE_ROWS, tn), lambda j, i, e, nu: (blk(i, nu), j)),
            scratch_shapes=[pltpu.VMEM((dm.dff, tn), BF16)]),
        out_shape=jax.ShapeDtypeStruct((n_buf, dm.d), F32),
        compiler_params=_params(("arbitrary", "arbitrary")),
        name="ffn_down",
    )(blk_e, n_used, hid, w_down)


def _route_kernel(lg_ref, rb_ref, o_ref, cnt_ref, carry_ref, *, n_exp, gsz):
    @pl.when(pl.program_id(0) == 0)
    def _():
        carry_ref[...] = jnp.zeros_like(carry_ref)

    tm = lg_ref.shape[0]
    lane_i = lax.broadcasted_iota(jnp.int32, (tm, LANES), 1)
    lane = lane_i.astype(F32)
    valid = lane_i < n_exp
    pos = lane_i % gsz
    neg = -jnp.inf
    big = float(LANES)
    scores = _sigmoid(lg_ref[...])
    biased = jnp.where(valid, scores + rb_ref[...], neg)

    def member(s):
        fwd = pltpu.roll(biased, LANES - s, 1)
        back = pltpu.roll(biased, gsz - s, 1)
        return jnp.where(pos + s < gsz, fwd, back)

    xs = [biased] + [member(s) for s in range(1, gsz)]
    gscore = None
    for a in range(gsz):
        for b in range(a + 1, gsz):
            pair = xs[a] + xs[b]
            gscore = pair if gscore is None else jnp.maximum(gscore, pair)
    gscore = jnp.where(valid, gscore, neg)
    gmax = jnp.max(gscore, axis=1, keepdims=True)
    grp = (lane_i // gsz).astype(F32)
    sel = jnp.min(jnp.where(gscore == gmax, grp, big), axis=1, keepdims=True)
    m = jnp.where((grp == sel) & valid, biased, neg)
    m1 = jnp.max(m, axis=1, keepdims=True)
    i1 = jnp.min(jnp.where(m == m1, lane, big), axis=1, keepdims=True)
    m = jnp.where(lane == i1, neg, m)
    m2 = jnp.max(m, axis=1, keepdims=True)
    i2 = jnp.min(jnp.where(m == m2, lane, big), axis=1, keepdims=True)
    oh1 = lane == i1
    oh2 = lane == i2
    s1 = jnp.sum(jnp.where(oh1, scores, 0.0), axis=1, keepdims=True)
    s2 = jnp.sum(jnp.where(oh2, scores, 0.0), axis=1, keepdims=True)
    both = oh1.astype(F32) + oh2.astype(F32)
    ri = lax.broadcasted_iota(jnp.int32, (tm, tm), 0)
    ci = lax.broadcasted_iota(jnp.int32, (tm, tm), 1)
    earlier = (ri > ci).astype(BF16)
    tot = carry_ref[...] + jnp.dot(earlier, both.astype(BF16), preferred_element_type=F32)
    r1 = jnp.sum(jnp.where(oh1, tot, 0.0), axis=1, keepdims=True)
    r2 = jnp.sum(jnp.where(oh2, tot, 0.0), axis=1, keepdims=True)
    carry_ref[...] += jnp.sum(both, axis=0, keepdims=True)
    cnt_ref[...] = carry_ref[...]
    out = jnp.zeros((tm, LANES), F32)
    for k, val in enumerate((i1, i2, r1, r2, s1 / (s1 + s2), s2 / (s1 + s2))):
        out = jnp.where(lane_i == k, val, out)
    o_ref[...] = out


def _route(dm, logits, router_b):
    n_tok = dm.t
    tm = _tile(256, n_tok)
    ne = logits.shape[1]
    rb = jnp.pad(router_b.astype(F32), (0, ne - dm.n_exp)).reshape(1, ne)
    sel, cnt = pl.pallas_call(
        functools.partial(_route_kernel, n_exp=dm.n_exp, gsz=dm.n_exp // ROUTER_GROUPS),
        grid=(n_tok // tm,),
        in_specs=[pl.BlockSpec((tm, ne), lambda i: (i, 0)), pl.BlockSpec((1, ne), lambda i: (0, 0))],
        out_specs=[pl.BlockSpec((tm, ne), lambda i: (i, 0)), pl.BlockSpec((1, ne), lambda i: (0, 0))],
        out_shape=[jax.ShapeDtypeStruct((n_tok, ne), F32), jax.ShapeDtypeStruct((1, ne), F32)],
        scratch_shapes=[pltpu.VMEM((1, ne), F32)],
        compiler_params=_params(("arbitrary",)),
        name="route",
    )(logits, rb)
    top_e = sel[:, 0:TOP_K].astype(jnp.int32)
    rank = sel[:, TOP_K:2 * TOP_K].astype(jnp.int32)
    top_w = sel[:, 2 * TOP_K:3 * TOP_K]
    counts = cnt[0, :dm.n_exp].astype(jnp.int32)
    padded = (counts + MOE_ROWS - 1) // MOE_ROWS * MOE_ROWS
    pend = jnp.cumsum(padded)
    pstart = pend - padded
    slot_dest = pstart[top_e] + rank
    n_slot = n_tok * TOP_K
    n_buf = -(-n_slot // MOE_ROWS) * MOE_ROWS + dm.n_exp * MOE_ROWS
    n_blk = n_buf // MOE_ROWS
    tok = jnp.repeat(jnp.arange(n_tok, dtype=jnp.int32), TOP_K)
    buf_tok = jnp.full((n_buf,), n_tok, dtype=jnp.int32).at[slot_dest.reshape(n_slot)].set(tok)
    blk_e = jnp.minimum(jnp.searchsorted(pend, jnp.arange(n_blk, dtype=jnp.int32) * MOE_ROWS, side='right'),
                        dm.n_exp - 1).astype(jnp.int32)
    n_used = (pend[-1] // MOE_ROWS).astype(jnp.int32).reshape(1)
    return top_w, buf_tok, blk_e, slot_dest, n_used


def _moe(dm, h_bf, logits, router_b, w_gate, w_up, w_down, l):
    top_w, buf_tok, blk_e, slot_dest, n_used = _route(dm, logits, router_b)
    x_pad = jnp.concatenate([h_bf, jnp.zeros((1, dm.d), h_bf.dtype)], axis=0)
    xb = x_pad[buf_tok]
    yb = _expert_ffn(dm, xb, blk_e, n_used, w_gate, w_up, w_down, l)
    return jnp.sum(yb[slot_dest] * top_w[..., None], axis=1)


def _grid_pos_embed(rows, d):
    nf = d // 4
    omega = 1.0 / (POS_BASE ** (jnp.arange(nf, dtype=F32) / nf))
    ang_r = jnp.arange(rows, dtype=F32)[:, None] * omega
    ang_c = jnp.arange(GRID_W, dtype=F32)[:, None] * omega
    emb_r = jnp.concatenate([jnp.sin(ang_r), jnp.cos(ang_r)], -1)
    emb_c = jnp.concatenate([jnp.sin(ang_c), jnp.cos(ang_c)], -1)
    emb = jnp.concatenate([jnp.broadcast_to(emb_r[:, None], (rows, GRID_W, d // 2)),
                           jnp.broadcast_to(emb_c[None], (rows, GRID_W, d // 2))], -1)
    return emb.reshape(rows * GRID_W, d)


def _ssd_gate_layouts(dm, dt, a_log):
    g_n, hpg = SSM_GROUPS, dm.hpg
    dt4 = dt.reshape(dm.t, 2, g_n, hpg)
    dtc = jnp.pad(dt4.transpose(1, 2, 0, 3), ((0, 0), (0, 0), (0, 0), (0, LANES - hpg)))
    dtr = dt4.transpose(1, 2, 3, 0)
    al = a_log.astype(F32).reshape(2, g_n, hpg)
    alc = jnp.pad(al[:, :, None, :], ((0, 0), (0, 0), (0, 0), (0, LANES - hpg)))
    alr = al[:, :, :, None]
    return dtc, dtr, alc, alr


def _gdn_gate_layouts(dm, g, beta, cl_list):
    j_n = dm.qkh
    g4 = g.reshape(dm.t, 2, j_n, 2).transpose(2, 0, 1, 3).reshape(j_n, dm.t, 4)
    b4 = beta.reshape(dm.t, 2, j_n, 2).transpose(2, 0, 1, 3).reshape(j_n, dm.t, 4)
    gb = jnp.concatenate([g4, b4], axis=-1)
    gcol = jnp.pad(gb, ((0, 0), (0, 0), (0, LANES - 8)))
    grows = []
    for (row_off, n_rows, cl) in cl_list:
        part = gb[:, row_off:row_off + n_rows]
        grows.append(part.reshape(j_n, n_rows // cl, cl, 8).transpose(0, 1, 3, 2))
    return gcol, grows


def kernel(x_prompt, x_sample, state_ssm, state_gdn, c, c_ctx, w_ada, b_ada, w_in, conv_ssm_w, conv_ssm_b, ssm_dt_bias, ssm_a_log, ssm_d, ssm_norm_w, conv_gdn_w, gdn_dt_bias, gdn_a_log, gdn_norm_w, w_branch_ssm, w_branch_gdn, w_out, ln1_g, ln1_b, ln2_g, ln2_b, router_w, router_b, w_gate, w_up, w_down):
    dm = _dims(x_prompt, x_sample, state_ssm, w_in, w_gate)
    alpha = (2.0 * dm.depth) ** 0.25

    cond = jnp.zeros((dm.ncond, dm.d), F32).at[0].set(c_ctx).at[1:1 + dm.db].set(c)
    mod4 = _ada_mod(dm, cond, w_ada, b_ada).reshape(dm.depth, dm.ncond, 1, 6 * dm.d)
    pos = _grid_pos_embed(dm.dseq // GRID_W, dm.d)
    x, h = _embed(dm, x_prompt.reshape(dm.tp, dm.d), x_sample.reshape(dm.t - dm.tp, dm.d), pos, mod4)

    rw_pad = jnp.pad(router_w, ((0, 0), (0, -router_w.shape[1] % LANES)))
    h0_ssm = state_ssm.reshape(dm.db, dm.depth, 2, SSM_GROUPS, dm.gw, dm.state)
    cl_ctx, cl_lat = min(SCAN_CHUNK, dm.seq), min(SCAN_CHUNK, dm.dseq)
    ssm_states, gdn_states = [], []
    for l in range(dm.depth):
        proj = _in_proj(dm, h, w_in, l)
        dt, g, beta = _gates(dm, proj[:, dm.off_dt:dm.off_dt + 2 * dm.heads],
                             proj[:, dm.off_a:dm.off_a + 2 * dm.vh], proj[:, dm.off_b:dm.off_b + 2 * dm.vh],
                             ssm_dt_bias[l], gdn_dt_bias[l], gdn_a_log[l])
        act_s = _conv_act(dm, proj, dm.off_xbc, dm.xbc, conv_ssm_w, conv_ssm_b, l)
        dtc, dtr, alc, alr = _ssd_gate_layouts(dm, dt, ssm_a_log[l])
        dskip = jnp.repeat(ssm_d[l].astype(F32), SSM_HEADDIM).reshape(1, dm.inner)
        nw_s = ssm_norm_w[l].reshape(1, dm.inner)
        y_ssm, hs = _ssd_call(dm, act_s, proj, dtc, dtr, alc, alr, dskip, nw_s, None, None, l, 0, dm.nb, dm.seq)
        (y_ssm,) = _ssd_call(dm, act_s, proj, dtc, dtr, alc, alr, dskip, nw_s, y_ssm, h0_ssm, l, dm.tp, dm.db,
                             dm.dseq)
        ssm_states.append(hs.reshape(dm.nb, 2, dm.heads, SSM_HEADDIM, dm.state))
        gcol, (grow_ctx, grow_lat) = _gdn_gate_layouts(
            dm, g, beta, [(0, dm.tp, cl_ctx), (dm.tp, dm.t - dm.tp, cl_lat)])
        nw_g = gdn_norm_w[l].reshape(1, GDN_DV)
        y_gdn, ss = _gdn_call(dm, proj, conv_gdn_w, gcol, grow_ctx, nw_g, None, None, l, 0, dm.nb, dm.seq)
        (y_gdn,) = _gdn_call(dm, proj, conv_gdn_w, gcol, grow_lat, nw_g, y_gdn, state_gdn, l, dm.tp, dm.db,
                             dm.dseq)
        gdn_states.append(ss)
        merged = _branch_merge(dm, y_ssm, y_gdn, w_branch_ssm, w_branch_gdn, proj, l)
        pre = _out_proj(dm, merged, w_out, x, mod4, l, alpha)
        x, h2, logits = _ln_router(dm, pre, ln1_g, ln1_b, mod4, rw_pad, l)
        y_moe = _moe(dm, h2, logits, router_b, w_gate, w_up, w_down, l)
        outs = _ln_res(dm, x, y_moe, ln2_g, ln2_b, mod4, l, alpha)
        x = outs[0]
        if l + 1 < dm.depth:
            h = outs[1]

    y_prompt = x[:dm.tp].reshape(dm.nb, dm.seq, dm.d)
    y_sample = x[dm.tp:].reshape(dm.db, dm.dseq, dm.d)
    return (y_prompt, y_sample, jnp.stack(ssm_states, axis=1), jnp.stack(gdn_states, axis=1))
```

```python
import functools
import math
from typing import NamedTuple

import jax
import jax.numpy as jnp
from jax import lax
from jax.experimental import pallas as pl
from jax.experimental.pallas import tpu as pltpu

F32 = jnp.float32
BF16 = jnp.bfloat16
HI = lax.Precision.HIGHEST

CONV_K = 5
SSM_HEADDIM = 64
SSM_GROUPS = 8
GDN_DK = 128
GDN_DV = 128
ROUTER_GROUPS = 8
TOP_K = 2
GRID_W = 64
POS_BASE = 10000.0
LN_EPS = 1e-5
RMS_EPS = 1e-6

LANES = 128
SUBLANES = 8
VMEM_LIMIT = 56 * 1024 * 1024

SCAN_CHUNK = 128
MOE_ROWS = 256


class Dims(NamedTuple):
    d: int
    depth: int
    nb: int
    seq: int
    db: int
    dseq: int
    tp: int
    t: int
    inner: int
    heads: int
    hpg: int
    gw: int
    state: int
    xbc: int
    qkh: int
    vh: int
    qkv: int
    val: int
    n_exp: int
    dff: int
    in_cols: int
    off_xbc: int
    off_dt: int
    off_qkv: int
    off_zg: int
    off_a: int
    off_b: int
    off_ga: int
    off_gb: int
    ncond: int


def _dims(x_prompt, x_sample, state_ssm, w_in, w_gate):
    nb, seq, d = x_prompt.shape
    db, dseq, _ = x_sample.shape
    depth = w_in.shape[0]
    inner = 2 * d
    heads = inner // SSM_HEADDIM
    state = state_ssm.shape[-1]
    xbc = inner + 2 * SSM_GROUPS * state
    qkh = d // GDN_DK
    vh = 2 * qkh
    key_dim = qkh * GDN_DK
    val = vh * GDN_DV
    qkv = 2 * key_dim + val
    off_xbc = inner
    off_dt = off_xbc + xbc
    off_qkv = off_dt + 2 * heads
    off_zg = off_qkv + qkv
    off_a = off_zg + val
    off_b = off_a + 2 * vh
    off_ga = off_b + 2 * vh
    off_gb = off_ga + d
    in_cols = off_gb + d
    assert in_cols == w_in.shape[2]
    ncond = -(-(1 + db) // SUBLANES) * SUBLANES
    return Dims(d=d, depth=depth, nb=nb, seq=seq, db=db, dseq=dseq, tp=nb * seq, t=nb * seq + db * dseq,
                inner=inner, heads=heads, hpg=heads // SSM_GROUPS, gw=inner // SSM_GROUPS, state=state, xbc=xbc,
                qkh=qkh, vh=vh, qkv=qkv, val=val, n_exp=w_gate.shape[1], dff=w_gate.shape[3], in_cols=in_cols,
                off_xbc=off_xbc, off_dt=off_dt, off_qkv=off_qkv, off_zg=off_zg, off_a=off_a, off_b=off_b,
                off_ga=off_ga, off_gb=off_gb, ncond=ncond)


def _tile(pref, *sizes):
    g = 0
    for s in sizes:
        g = math.gcd(g, s)
    t = math.gcd(pref, g)
    return t


def _params(sem, vmem=VMEM_LIMIT):
    return pltpu.CompilerParams(dimension_semantics=sem, vmem_limit_bytes=vmem)


def _sigmoid(x):
    return 1.0 / (1.0 + jnp.exp(-x))


def _silu(x):
    return x * _sigmoid(x)


def _softplus(x):
    return jnp.maximum(x, 0.0) + jnp.log(1.0 + jnp.exp(-jnp.abs(x)))


def _cond_of_row(dm, row0):
    return jnp.where(row0 < dm.tp, 0, 1 + (row0 - dm.tp) // dm.dseq)


def _ada_kernel(c_ref, w_ref, b_ref, o_ref):
    s = _silu(c_ref[...]).astype(BF16)
    o_ref[...] = jnp.dot(s, w_ref[...].astype(BF16), preferred_element_type=F32) + b_ref[...]


def _ada_mod(dm, cond, w_ada, b_ada):
    n = w_ada.shape[2]
    tn = _tile(512, n)
    return pl.pallas_call(
        _ada_kernel,
        grid=(dm.depth, n // tn),
        in_specs=[pl.BlockSpec((dm.ncond, dm.d), lambda l, j: (0, 0)),
                  pl.BlockSpec((None, dm.d, tn), lambda l, j: (l, 0, j)),
                  pl.BlockSpec((None, 1, tn), lambda l, j: (l, 0, j))],
        out_specs=pl.BlockSpec((None, dm.ncond, tn), lambda l, j: (l, 0, j)),
        out_shape=jax.ShapeDtypeStruct((dm.depth, dm.ncond, n), F32),
        compiler_params=_params(("arbitrary", "arbitrary")),
        name="ada_mod",
    )(cond, w_ada, b_ada.reshape(dm.depth, 1, n))


def _mod_spec(dm, tm, l, which, width=None, col_of=None):
    width = dm.d if width is None else width
    per = dm.d // width

    def idx(*g):
        i, j = (g[0], 0) if col_of is None else col_of(*g)
        return (l, _cond_of_row(dm, i * tm), 0, which * per + j)

    return pl.BlockSpec((None, None, 1, width), idx)


def _embed_kernel(xp_ref, xs_ref, pos_ref, sh_ref, sc_ref, x_ref, h_ref, *, npb):
    is_ctx = pl.program_id(0) < npb
    x = jnp.where(is_ctx, xp_ref[...], xs_ref[...] + pos_ref[...])
    x_ref[...] = x
    h_ref[...] = (x * (1.0 + sc_ref[...]) + sh_ref[...]).astype(BF16)


def _embed(dm, xp, xs, pos, mod4):
    tm = _tile(256, dm.seq, dm.dseq)
    npb = dm.tp // tm
    nsb = (dm.t - dm.tp) // tm
    ppb = dm.dseq // tm
    row = lambda i: (i, 0)
    return pl.pallas_call(
        functools.partial(_embed_kernel, npb=npb),
        grid=(dm.t // tm,),
        in_specs=[pl.BlockSpec((tm, dm.d), lambda i: (jnp.minimum(i, npb - 1), 0)),
                  pl.BlockSpec((tm, dm.d), lambda i: (jnp.clip(i - npb, 0, nsb - 1), 0)),
                  pl.BlockSpec((tm, dm.d), lambda i: (jnp.maximum(i - npb, 0) % ppb, 0)),
                  _mod_spec(dm, tm, 0, 0), _mod_spec(dm, tm, 0, 1)],
        out_specs=[pl.BlockSpec((tm, dm.d), row), pl.BlockSpec((tm, dm.d), row)],
        out_shape=[jax.ShapeDtypeStruct((dm.t, dm.d), F32), jax.ShapeDtypeStruct((dm.t, dm.d), BF16)],
        compiler_params=_params(("arbitrary",)),
        name="embed_mod",
    )(xp, xs, pos, mod4, mod4)


def _mm_res_kernel(x_ref, w_ref, r_ref, g_ref, o_ref, wb_ref, *, alpha):
    @pl.when(pl.program_id(1) == 0)
    def _():
        wb_ref[...] = w_ref[...].astype(BF16)

    acc = jnp.dot(x_ref[...], wb_ref[...], preferred_element_type=F32)
    o_ref[...] = alpha * r_ref[...] + g_ref[...] * acc


def _mm_stream_kernel(x_ref, w_ref, o_ref):
    o_ref[...] = jnp.dot(x_ref[...], w_ref[...].astype(BF16), preferred_element_type=F32).astype(o_ref.dtype)


def _in_proj(dm, h, w_in, l):
    k = dm.d
    n = dm.in_cols
    tm = _tile(2048, dm.t)
    tn = _tile(256, n)
    return pl.pallas_call(
        _mm_stream_kernel,
        grid=(dm.t // tm, n // tn),
        in_specs=[pl.BlockSpec((tm, k), lambda i, j: (i, 0)),
                  pl.BlockSpec((None, k, tn), lambda i, j: (l, 0, j))],
        out_specs=pl.BlockSpec((tm, tn), lambda i, j: (i, j)),
        out_shape=jax.ShapeDtypeStruct((dm.t, n), F32),
        compiler_params=_params(("arbitrary", "arbitrary")),
        name="in_proj",
    )(h, w_in)


def _out_proj(dm, merged, w_out, x, mod4, l, alpha):
    k = dm.d
    tm = _tile(1024, dm.tp, dm.dseq)
    tn = _tile(512, dm.d)
    return pl.pallas_call(
        functools.partial(_mm_res_kernel, alpha=alpha),
        grid=(dm.d // tn, dm.t // tm),
        in_specs=[pl.BlockSpec((tm, k), lambda j, i: (i, 0)),
                  pl.BlockSpec((None, k, tn), lambda j, i: (l, 0, j)),
                  pl.BlockSpec((tm, tn), lambda j, i: (i, j)),
                  _mod_spec(dm, tm, l, 2, width=tn, col_of=lambda j, i: (i, j))],
        out_specs=pl.BlockSpec((tm, tn), lambda j, i: (i, j)),
        out_shape=jax.ShapeDtypeStruct((dm.t, dm.d), F32),
        scratch_shapes=[pltpu.VMEM((k, tn), BF16)],
        compiler_params=_params(("arbitrary", "arbitrary")),
        name="out_proj",
    )(merged, w_out, x, mod4)


def _branch_kernel(ya_ref, yb_ref, wa_ref, wb_ref, ga_ref, gb_ref, o_ref, acc_ref, res_ref, wpa_ref, wpb_ref,
                   *, nk, tk):
    i = pl.program_id(1)
    kb = pl.program_id(2)

    @pl.when(kb % nk == 0)
    def _():
        acc_ref[...] = jnp.zeros_like(acc_ref)

    def accumulate(y_ref, w_ref, wp_ref, kk):
        rows = pl.ds(pl.multiple_of(kk * tk, tk), tk)

        @pl.when(i == 0)
        def _():
            wp_ref[rows, :] = w_ref[...].astype(BF16)

        acc_ref[...] += jnp.dot(y_ref[...], wp_ref[rows, :], preferred_element_type=F32)

    @pl.when(kb < nk)
    def _():
        accumulate(ya_ref, wa_ref, wpa_ref, kb)

    @pl.when(kb >= nk)
    def _():
        accumulate(yb_ref, wb_ref, wpb_ref, kb - nk)

    @pl.when(kb == nk - 1)
    def _():
        res_ref[...] = _sigmoid(ga_ref[...]) * acc_ref[...]

    @pl.when(kb == 2 * nk - 1)
    def _():
        o_ref[...] = (res_ref[...] + _sigmoid(gb_ref[...]) * acc_ref[...]).astype(o_ref.dtype)


def _branch_merge(dm, y_ssm, y_gdn, w_a, w_b, proj, l):
    assert dm.inner == dm.val
    kdim = dm.inner
    tm = _tile(1024, dm.t)
    tn = _tile(512, dm.d, dm.off_ga, dm.off_gb)
    tk = _tile(1024, kdim)
    nk = kdim // tk
    ca, cb = dm.off_ga // tn, dm.off_gb // tn

    def wa_idx(j, i, kb):
        return (l, jnp.where(i == 0, jnp.minimum(kb, nk - 1), nk - 1), j)

    def wb_idx(j, i, kb):
        return (l, jnp.where(i == 0, jnp.maximum(kb - nk, 0), nk - 1), j)

    return pl.pallas_call(
        functools.partial(_branch_kernel, nk=nk, tk=tk),
        grid=(dm.d // tn, dm.t // tm, 2 * nk),
        in_specs=[pl.BlockSpec((tm, tk), lambda j, i, kb: (i, jnp.minimum(kb, nk - 1))),
                  pl.BlockSpec((tm, tk), lambda j, i, kb: (i, jnp.maximum(kb - nk, 0))),
                  pl.BlockSpec((None, tk, tn), wa_idx),
                  pl.BlockSpec((None, tk, tn), wb_idx),
                  pl.BlockSpec((tm, tn), lambda j, i, kb: (i, ca + j)),
                  pl.BlockSpec((tm, tn), lambda j, i, kb: (i, cb + j))],
        out_specs=pl.BlockSpec((tm, tn), lambda j, i, kb: (i, j)),
        out_shape=jax.ShapeDtypeStruct((dm.t, dm.d), BF16),
        scratch_shapes=[pltpu.VMEM((tm, tn), F32), pltpu.VMEM((tm, tn), F32),
                        pltpu.VMEM((kdim, tn), BF16), pltpu.VMEM((kdim, tn), BF16)],
        compiler_params=_params(("arbitrary", "arbitrary", "arbitrary")),
        name="branch_merge",
    )(y_ssm, y_gdn, w_a, w_b, proj, proj)


def _gates_kernel(dtr_ref, ar_ref, br_ref, dtb_ref, gb_ref, al_ref, dt_ref, g_ref, beta_ref):
    dt_ref[...] = _softplus(dtr_ref[...] + dtb_ref[...])
    g_ref[...] = -jnp.exp(al_ref[...]) * _softplus(ar_ref[...] + gb_ref[...])
    beta_ref[...] = _sigmoid(br_ref[...])


def _gates(dm, dt_raw, a_raw, b_raw, dt_bias, g_bias, a_log):
    tm = _tile(512, dm.t)
    w1, w2 = 2 * dm.heads, 2 * dm.vh
    row = lambda i: (i, 0)
    fix = lambda i: (0, 0)
    return pl.pallas_call(
        _gates_kernel,
        grid=(dm.t // tm,),
        in_specs=[pl.BlockSpec((tm, w1), row), pl.BlockSpec((tm, w2), row), pl.BlockSpec((tm, w2), row),
                  pl.BlockSpec((1, w1), fix), pl.BlockSpec((1, w2), fix), pl.BlockSpec((1, w2), fix)],
        out_specs=[pl.BlockSpec((tm, w1), row), pl.BlockSpec((tm, w2), row), pl.BlockSpec((tm, w2), row)],
        out_shape=[jax.ShapeDtypeStruct((dm.t, w1), F32), jax.ShapeDtypeStruct((dm.t, w2), F32),
                   jax.ShapeDtypeStruct((dm.t, w2), F32)],
        compiler_params=_params(("arbitrary",)),
        name="gate_prep",
    )(dt_raw, a_raw, b_raw, dt_bias.reshape(1, w1), g_bias.reshape(1, w2), a_log.reshape(1, w2))


def _conv_kernel(*refs, rows, blocks_ctx, blocks_per_ctx_seq, blocks_per_lat_seq, has_bias):
    if has_bias:
        x_ref, prev_ref, next_ref, w_ref, b_ref, o_ref = refs
    else:
        x_ref, prev_ref, next_ref, w_ref, o_ref = refs
        b_ref = None
    i = pl.program_id(0)
    in_ctx = i < blocks_ctx
    pos = jnp.where(in_ctx, i % blocks_per_ctx_seq, (i - blocks_ctx) % blocks_per_lat_seq)
    per = jnp.where(in_ctx, blocks_per_ctx_seq, blocks_per_lat_seq)
    prev = jnp.where(pos == 0, 0.0, prev_ref[...])
    nxt = jnp.where(pos == per - 1, 0.0, next_ref[...])
    ext = jnp.concatenate([prev, x_ref[...], nxt], axis=0)
    n = rows + 2 * SUBLANES
    pad = CONV_K // 2
    w = w_ref[...]
    acc = None
    for tap in range(CONV_K):
        start = SUBLANES - pad + tap
        shifted = ext if start == 0 else pltpu.roll(ext, n - start, 0)
        term = shifted[:rows] * w[tap:tap + 1, :]
        acc = term if acc is None else acc + term
    if has_bias:
        acc = acc + b_ref[...]
    o_ref[...] = _silu(acc)


def _conv_act(dm, proj, col_off, width, w, b, l):
    rows = _tile(256, dm.seq, dm.dseq)
    tc = _tile(2048, col_off, width)
    c0 = col_off // tc
    hb = rows // SUBLANES
    nhalo = dm.t // SUBLANES
    has_bias = b is not None
    in_specs = [pl.BlockSpec((rows, tc), lambda i, j: (i, c0 + j)),
                pl.BlockSpec((SUBLANES, tc), lambda i, j: (jnp.maximum(i * hb - 1, 0), c0 + j)),
                pl.BlockSpec((SUBLANES, tc), lambda i, j: (jnp.minimum((i + 1) * hb, nhalo - 1), c0 + j)),
                pl.BlockSpec((None, CONV_K, tc), lambda i, j: (l, 0, j))]
    args = [proj, proj, proj, w]
    if has_bias:
        in_specs.append(pl.BlockSpec((None, 1, tc), lambda i, j: (l, 0, j)))
        args.append(b.reshape(dm.depth, 1, width))
    return pl.pallas_call(
        functools.partial(_conv_kernel, rows=rows, blocks_ctx=dm.tp // rows, blocks_per_ctx_seq=dm.seq // rows,
                          blocks_per_lat_seq=dm.dseq // rows, has_bias=has_bias),
        grid=(dm.t // rows, width // tc),
        in_specs=in_specs,
        out_specs=pl.BlockSpec((rows, tc), lambda i, j: (i, j)),
        out_shape=jax.ShapeDtypeStruct((dm.t, width), F32),
        compiler_params=_params(("arbitrary", "arbitrary")),
        name="conv_act",
    )(*args)


def _ssd_kernel(*refs, nc, hpg, hd, has_state, want_state):
    it = iter(refs)
    xs_ref, b_ref, c_ref, dtr_ref, alr_ref, z_ref, dsk_ref, nw_ref = (next(it) for _ in range(8))
    h0_ref = next(it) if has_state else None
    y_ref = next(it)
    hout_ref = next(it) if want_state else None
    ht_ref, yf_ref, xw_ref, yc_ref = (next(it) for _ in range(4))

    d = pl.program_id(2)
    c = pl.program_id(3)
    q = xs_ref.shape[0]
    fwd = d == 0
    ii = lax.broadcasted_iota(jnp.int32, (q, q), 0)
    jj = lax.broadcasted_iota(jnp.int32, (q, q), 1)
    mask = ((ii - jj) * (1 - 2 * d)) >= 0
    tri = mask.astype(F32)

    dt_r = dtr_ref[...]
    a_r = dt_r * (-jnp.exp(alr_ref[...]))
    acum_r = lax.dot_general(a_r, tri, (((1,), (1,)), ((), ())), precision=HI, preferred_element_type=F32)
    alast_r = jnp.where(fwd, acum_r[:, q - 1:q], acum_r[:, 0:1])
    assert q == LANES and 2 * hpg <= LANES
    cols = jnp.concatenate([acum_r, dt_r, jnp.zeros((LANES - 2 * hpg, q), F32)], axis=0).T
    acum_c = cols[:, :hpg]
    dt_c = cols[:, hpg:2 * hpg]
    alast_c = jnp.where(fwd, acum_c[q - 1:q, :], acum_c[0:1, :])

    bm = b_ref[...].astype(BF16)
    cm = c_ref[...].astype(BF16)
    cb = lax.dot_general(cm, bm, (((1,), (1,)), ((), ())), preferred_element_type=F32)

    @pl.when(c == 0)
    def _():
        if has_state:
            ht_ref[...] = h0_ref[...].T
        else:
            ht_ref[...] = jnp.zeros_like(ht_ref)

    ht = ht_ref[...]
    yo = jnp.dot(cm, ht.astype(BF16), preferred_element_type=F32)
    xs = xs_ref[...]
    log_end = alast_c - acum_c + jnp.log(dt_c)
    for r in range(hpg):
        sl = slice(r * hd, (r + 1) * hd)
        ac = jnp.broadcast_to(acum_c[:, r:r + 1], (q, q))
        ar = acum_r[r:r + 1, :]
        dec = jnp.where(mask, jnp.exp(ac - ar), 0.0)
        wgt = (cb * dec * dt_r[r:r + 1, :]).astype(BF16)
        x_r = xs[:, sl]
        yd = jnp.dot(wgt, x_r.astype(BF16), preferred_element_type=F32)
        yc_ref[:, sl] = yd + yo[:, sl] * jnp.exp(ac[:, :hd])
        xw_ref[:, sl] = (x_r * jnp.exp(jnp.broadcast_to(log_end[:, r:r + 1], (q, hd)))).astype(BF16)
    st = lax.dot_general(bm, xw_ref[...], (((0,), (0,)), ((), ())), preferred_element_type=F32)
    for r in range(hpg):
        sl = slice(r * hd, (r + 1) * hd)
        ht_ref[:, sl] = ht[:, sl] * jnp.exp(alast_r[r:r + 1, :]) + st[:, sl]

    cidx = c + d * (nc - 1 - 2 * c)
    rows = pl.ds(pl.multiple_of(cidx * q, q), q)

    @pl.when(fwd)
    def _():
        yf_ref[rows, :] = yc_ref[...]

    @pl.when(d == 1)
    def _():
        tot = yf_ref[rows, :] + yc_ref[...] + dsk_ref[...] * xs
        gated = tot * _silu(z_ref[...])
        ms = jnp.mean(gated * gated, axis=-1, keepdims=True)
        y_ref[...] = (gated * lax.rsqrt(ms + RMS_EPS) * nw_ref[...]).astype(y_ref.dtype)

    if want_state:
        @pl.when(c == nc - 1)
        def _():
            hout_ref[...] = ht_ref[...].T


def _ssd_call(dm, act, proj, dtr, alr, dskip, nw, y_prev, h0, l, row_off, nseq, length):
    q = min(SCAN_CHUNK, length)
    nc = length // q
    base = row_off // q
    g_n, gw, n = SSM_GROUPS, dm.gw, dm.state
    has_state = h0 is not None
    want_state = not has_state
    bcol = dm.inner // n

    def rb(b, d, c):
        return base + b * nc + c + d * (nc - 1 - 2 * c)

    def rb_out(b, d, c):
        return base + b * nc + jnp.where(d == 0, nc - 1, nc - 1 - c)

    in_specs = [pl.BlockSpec((q, gw), lambda b, g, d, c: (rb(b, d, c), g)),
                pl.BlockSpec((q, n), lambda b, g, d, c: (rb(b, d, c), bcol + g)),
                pl.BlockSpec((q, n), lambda b, g, d, c: (rb(b, d, c), bcol + g_n + g)),
                pl.BlockSpec((None, None, dm.hpg, q), lambda b, g, d, c: (d, g, 0, rb(b, d, c))),
                pl.BlockSpec((None, None, dm.hpg, 1), lambda b, g, d, c: (d, g, 0, 0)),
                pl.BlockSpec((q, gw), lambda b, g, d, c: (rb_out(b, d, c), g)),
                pl.BlockSpec((1, gw), lambda b, g, d, c: (0, g)),
                pl.BlockSpec((1, gw), lambda b, g, d, c: (0, g))]
    args = [act, act, act, dtr, alr, proj, dskip, nw]
    if has_state:
        in_specs.append(pl.BlockSpec((None, None, None, None, gw, n), lambda b, g, d, c: (b, l, d, g, 0, 0)))
        args.append(h0)
    out_specs = [pl.BlockSpec((q, gw), lambda b, g, d, c: (rb_out(b, d, c), g))]
    out_shape = [jax.ShapeDtypeStruct((dm.t, dm.inner), BF16)]
    if want_state:
        out_specs.append(pl.BlockSpec((None, None, None, gw, n), lambda b, g, d, c: (b, d, g, 0, 0)))
        out_shape.append(jax.ShapeDtypeStruct((nseq, 2, g_n, gw, n), F32))
    aliases = {}
    if y_prev is not None:
        in_specs.append(pl.BlockSpec(memory_space=pl.ANY))
        args.append(y_prev)
        aliases = {len(args) - 1: 0}

    def body(*refs):
        refs = list(refs)
        if y_prev is not None:
            n_in = len(args)
            del refs[n_in - 1]
        _ssd_kernel(*refs, nc=nc, hpg=dm.hpg, hd=SSM_HEADDIM, has_state=has_state, want_state=want_state)

    return pl.pallas_call(
        body,
        grid=(nseq, g_n, 2, nc),
        in_specs=in_specs,
        out_specs=out_specs,
        out_shape=out_shape,
        scratch_shapes=[pltpu.VMEM((n, gw), F32), pltpu.VMEM((length, gw), F32),
                        pltpu.VMEM((q, gw), BF16), pltpu.VMEM((q, gw), F32)],
        input_output_aliases=aliases,
        compiler_params=_params(("arbitrary",) * 4),
        name="ssd_scan",
    )(*args)


def _conv_silu_seq(x, w):
    n = x.shape[0]
    pad = CONV_K // 2
    t = lax.broadcasted_iota(jnp.int32, x.shape, 0)
    acc = x * w[pad:pad + 1, :]
    for tap in range(CONV_K):
        off = tap - pad
        if off == 0:
            continue
        shifted = pltpu.roll(x, (-off) % n, 0)
        inside = (t >= -off) if off < 0 else (t < n - off)
        acc = acc + jnp.where(inside, shifted, 0.0) * w[tap:tap + 1, :]
    return _silu(acc)


def _gdn_kernel(*refs, nc, cl, has_state, want_state):
    it = iter(refs)
    q_ref, k_ref, v0_ref, v1_ref, wq_ref, wk_ref, wv0_ref, wv1_ref = (next(it) for _ in range(8))
    z0_ref, z1_ref, grow_ref, nw_ref = (next(it) for _ in range(4))
    s0_ref = next(it) if has_state else None
    y_ref = next(it)
    sout_ref = next(it) if want_state else None
    qn_ref, kn_ref, va_ref, t_ref, qk_ref, gc_ref, o_ref, s_ref = (next(it) for _ in range(8))
    z_refs = (z0_ref, z1_ref)
    dk = q_ref.shape[1]
    dv = v0_ref.shape[1]

    qv = _conv_silu_seq(q_ref[...], wq_ref[...])
    qn_ref[...] = (qv * lax.rsqrt(jnp.sum(qv * qv, -1, keepdims=True) + RMS_EPS) * (dk ** -0.5)).astype(BF16)
    kv = _conv_silu_seq(k_ref[...], wk_ref[...])
    kn_ref[...] = (kv * lax.rsqrt(jnp.sum(kv * kv, -1, keepdims=True) + RMS_EPS)).astype(BF16)
    va_ref[0] = _conv_silu_seq(v0_ref[...], wv0_ref[...])
    va_ref[1] = _conv_silu_seq(v1_ref[...], wv1_ref[...])

    ii = lax.broadcasted_iota(jnp.int32, (cl, cl), 0)
    jj = lax.broadcasted_iota(jnp.int32, (cl, cl), 1)
    eye = (ii == jj).astype(F32)
    tri_f = (ii >= jj).astype(F32)
    incl = (ii >= jj, ii <= jj)
    strict = (ii > jj, ii < jj)
    assert cl == LANES
    sub = lax.broadcasted_iota(jnp.int32, (SUBLANES, cl), 0)
    n_lvl = int(math.log2(cl))
    pair_mask = [((ii >> (k + 1)) == (jj >> (k + 1))) & ((ii >> k) != (jj >> k)) for k in range(n_lvl)]
    nt = (((1,), (1,)), ((), ()))

    ua = 4 if nc % 4 == 0 else (2 if nc % 2 == 0 else 1)

    def phase_a(t, carry):
        nms, tms, slots = [], [], []
        for u in range(ua):
            ci = t * ua + u
            rows = pl.ds(pl.multiple_of(ci * cl, cl), cl)
            kc = kn_ref[rows, :]
            qc = qn_ref[rows, :]
            kk = lax.dot_general(kc, kc, nt, preferred_element_type=F32)
            qk = lax.dot_general(qc, kc, nt, preferred_element_type=F32)
            grow = grow_ref[ci]
            pre_r = lax.dot_general(grow, tri_f, nt, precision=HI, preferred_element_type=F32)
            suf_r = jnp.sum(grow, axis=1, keepdims=True) - pre_r + grow
            grb = jnp.where(sub >= 4, grow, jnp.where(sub >= 2, suf_r, pre_r))
            gcb = jnp.concatenate([grb, jnp.zeros((LANES - SUBLANES, cl), F32)], axis=0).T
            gc_ref[rows, :] = gcb
            for kx in range(4):
                dd = kx // 2
                gi = gcb[:, kx:kx + 1]
                gj = grb[kx:kx + 1, :]
                bi = gcb[:, 4 + kx:5 + kx]
                dec = jnp.where(incl[dd], jnp.exp(gi - gj), 0.0)
                nm = jnp.where(strict[dd], bi * kk * dec, 0.0)
                qk_ref[ci, kx] = (qk * dec).astype(BF16)
                nms.append(nm)
                tms.append(eye - jnp.where(pair_mask[0], nm, 0.0))
                slots.append((ci, kx))
        for lvl in range(1, n_lvl):
            tbs = [tm.astype(BF16) for tm in tms]
            cts = [jnp.dot(jnp.where(pair_mask[lvl], nm, 0.0).astype(BF16), tb, preferred_element_type=F32)
                   for nm, tb in zip(nms, tbs)]
            tms = [tm - jnp.dot(tb, ct.astype(BF16), preferred_element_type=F32)
                   for tm, tb, ct in zip(tms, tbs, cts)]
        for (ci, kx), tm in zip(slots, tms):
            t_ref[ci, kx] = tm.astype(BF16)
        return carry

    lax.fori_loop(0, nc // ua, phase_a, 0)

    for kx in range(4):
        if has_state:
            s_ref[kx] = s0_ref[kx // 2, kx % 2]
        else:
            s_ref[kx] = jnp.zeros((dk, dv), F32)

    def phase_b(t, carry):
        cidx = (t, t, nc - 1 - t, nc - 1 - t)
        rows = [pl.ds(pl.multiple_of(ci * cl, cl), cl) for ci in cidx]
        kcs = [kn_ref[r, :] for r in rows]
        ss = [s_ref[kx] for kx in range(4)]
        gis = [gc_ref[rows[kx], :][:, kx:kx + 1] for kx in range(4)]
        bis = [gc_ref[rows[kx], :][:, 4 + kx:5 + kx] for kx in range(4)]
        egis = [jnp.exp(gi) for gi in gis]
        kqs = [jnp.dot(jnp.concatenate([kcs[kx], qn_ref[rows[kx], :]], axis=0), ss[kx].astype(BF16),
                       preferred_element_type=F32) for kx in range(4)]
        rhs = [(bis[kx] * (va_ref[kx % 2, rows[kx], :] - egis[kx] * kqs[kx][:cl])).astype(BF16) for kx in range(4)]
        vnews = [jnp.dot(t_ref[cidx[kx], kx], rhs[kx], preferred_element_type=F32) for kx in range(4)]
        for kx in range(4):
            o_ref[kx // 2, kx % 2, rows[kx], :] = egis[kx] * kqs[kx][cl:] + jnp.dot(
                qk_ref[cidx[kx], kx], vnews[kx].astype(BF16), preferred_element_type=F32)
        glasts = [gis[kx][cl - 1:cl, :] if kx < 2 else gis[kx][0:1, :] for kx in range(4)]
        vss = [(jnp.exp(glasts[kx] - gis[kx]) * vnews[kx]).astype(BF16) for kx in range(4)]
        for kx in range(4):
            s_ref[kx] = ss[kx] * jnp.exp(glasts[kx]) + lax.dot_general(
                kcs[kx], vss[kx], (((0,), (0,)), ((), ())), preferred_element_type=F32)
        return carry

    lax.fori_loop(0, nc, phase_b, 0)

    for e in range(2):
        o = o_ref[0, e] + o_ref[1, e]
        on = o * lax.rsqrt(jnp.mean(o * o, -1, keepdims=True) + RMS_EPS)
        y_ref[:, e * dv:(e + 1) * dv] = (on * nw_ref[...] * _silu(z_refs[e][...])).astype(y_ref.dtype)
    if want_state:
        for kx in range(4):
            sout_ref[kx // 2, kx % 2] = s_ref[kx]


def _gdn_call(dm, proj, conv_w, grow, nw, y_prev, s0, l, row_off, nseq, length):
    cl = min(SCAN_CHUNK, length)
    nc = length // cl
    assert row_off % length == 0
    base = row_off // length
    dk, dv = GDN_DK, GDN_DV
    has_state = s0 is not None
    want_state = not has_state
    assert dm.off_qkv % dk == 0 and dm.off_zg % dv == 0
    qcol = dm.off_qkv // dk
    kcol = dm.qkh
    vcol = 2 * dm.qkh
    zcol = dm.off_zg // dv

    def col_specs(first, shape, row_of):
        return [pl.BlockSpec(shape, lambda b, j: row_of(b) + (first + j,)),
                pl.BlockSpec(shape, lambda b, j: row_of(b) + (first + kcol + j,)),
                pl.BlockSpec(shape, lambda b, j: row_of(b) + (first + vcol + 2 * j,)),
                pl.BlockSpec(shape, lambda b, j: row_of(b) + (first + vcol + 2 * j + 1,))]

    in_specs = (col_specs(qcol, (length, dk), lambda b: (base + b,))
                + col_specs(0, (None, CONV_K, dk), lambda b: (l, 0))
                + [pl.BlockSpec((length, dv), lambda b, j: (base + b, zcol + 2 * j)),
                   pl.BlockSpec((length, dv), lambda b, j: (base + b, zcol + 2 * j + 1)),
                   pl.BlockSpec((None, nc, SUBLANES, cl), lambda b, j: (j, b, 0, 0)),
                   pl.BlockSpec((1, dv), lambda b, j: (0, 0))])
    args = [proj] * 4 + [conv_w] * 4 + [proj, proj, grow, nw]
    if has_state:
        in_specs.append(pl.BlockSpec((None, None, 2, 2, dk, dv), lambda b, j: (b, l, 0, j, 0, 0)))
        args.append(s0)
    out_specs = [pl.BlockSpec((length, 2 * dv), lambda b, j: (base + b, j))]
    out_shape = [jax.ShapeDtypeStruct((dm.t, dm.val), BF16)]
    if want_state:
        out_specs.append(pl.BlockSpec((None, 2, 2, dk, dv), lambda b, j: (b, 0, j, 0, 0)))
        out_shape.append(jax.ShapeDtypeStruct((nseq, 2, dm.vh, dk, dv), F32))
    aliases = {}
    if y_prev is not None:
        in_specs.append(pl.BlockSpec(memory_space=pl.ANY))
        args.append(y_prev)
        aliases = {len(args) - 1: 0}

    def body(*refs):
        refs = list(refs)
        if y_prev is not None:
            del refs[len(args) - 1]
        _gdn_kernel(*refs, nc=nc, cl=cl, has_state=has_state, want_state=want_state)

    return pl.pallas_call(
        body,
        grid=(nseq, dm.qkh),
        in_specs=in_specs,
        out_specs=out_specs,
        out_shape=out_shape,
        scratch_shapes=[pltpu.VMEM((length, dk), BF16), pltpu.VMEM((length, dk), BF16),
                        pltpu.VMEM((2, length, dv), F32),
                        pltpu.VMEM((nc, 4, cl, cl), BF16), pltpu.VMEM((nc, 4, cl, cl), BF16),
                        pltpu.VMEM((length, LANES), F32),
                        pltpu.VMEM((2, 2, length, dv), F32), pltpu.VMEM((4, dk, dv), F32)],
        input_output_aliases=aliases,
        compiler_params=_params(("arbitrary", "arbitrary")),
        name="gdn_scan",
    )(*args)


def _layer_norm(x, g, b):
    mu = jnp.mean(x, -1, keepdims=True)
    xc = x - mu
    var = jnp.mean(xc * xc, -1, keepdims=True)
    return xc * lax.rsqrt(var + LN_EPS) * g + b


def _ln_router_kernel(pre_ref, g_ref, b_ref, sh_ref, sc_ref, rw_ref, x_ref, h_ref, lg_ref):
    x = _layer_norm(pre_ref[...], g_ref[...], b_ref[...])
    x_ref[...] = x
    h = x * (1.0 + sc_ref[...]) + sh_ref[...]
    h_ref[...] = h.astype(BF16)
    lg_ref[...] = jnp.dot(h, rw_ref[...], precision=HI, preferred_element_type=F32)


def _ln_router(dm, pre, g, b, mod4, rw_pad, l):
    tm = _tile(256, dm.seq, dm.dseq)
    row = lambda i: (i, 0)
    fix = lambda i: (0, 0)
    ne = rw_pad.shape[1]
    return pl.pallas_call(
        _ln_router_kernel,
        grid=(dm.t // tm,),
        in_specs=[pl.BlockSpec((tm, dm.d), row),
                  pl.BlockSpec((None, 1, dm.d), lambda i: (l, 0, 0)),
                  pl.BlockSpec((None, 1, dm.d), lambda i: (l, 0, 0)),
                  _mod_spec(dm, tm, l, 3), _mod_spec(dm, tm, l, 4),
                  pl.BlockSpec((dm.d, ne), fix)],
        out_specs=[pl.BlockSpec((tm, dm.d), row), pl.BlockSpec((tm, dm.d), row), pl.BlockSpec((tm, ne), row)],
        out_shape=[jax.ShapeDtypeStruct((dm.t, dm.d), F32), jax.ShapeDtypeStruct((dm.t, dm.d), BF16),
                   jax.ShapeDtypeStruct((dm.t, ne), F32)],
        compiler_params=_params(("arbitrary",)),
        name="ln1_router",
    )(pre, g.reshape(dm.depth, 1, dm.d), b.reshape(dm.depth, 1, dm.d), mod4, mod4, rw_pad)


def _ln_res_kernel(*refs, alpha, with_mod):
    if with_mod:
        x_ref, y_ref, gt_ref, g_ref, b_ref, sh_ref, sc_ref, o_ref, h_ref = refs
    else:
        x_ref, y_ref, gt_ref, g_ref, b_ref, o_ref = refs
    x = _layer_norm(alpha * x_ref[...] + gt_ref[...] * y_ref[...], g_ref[...], b_ref[...])
    o_ref[...] = x
    if with_mod:
        h_ref[...] = (x * (1.0 + sc_ref[...]) + sh_ref[...]).astype(BF16)


def _ln_res(dm, x, y, g, b, mod4, l, alpha):
    tm = _tile(256, dm.seq, dm.dseq)
    row = lambda i: (i, 0)
    with_mod = l + 1 < dm.depth
    in_specs = [pl.BlockSpec((tm, dm.d), row), pl.BlockSpec((tm, dm.d), row), _mod_spec(dm, tm, l, 5),
                pl.BlockSpec((None, 1, dm.d), lambda i: (l, 0, 0)),
                pl.BlockSpec((None, 1, dm.d), lambda i: (l, 0, 0))]
    args = [x, y, mod4, g.reshape(dm.depth, 1, dm.d), b.reshape(dm.depth, 1, dm.d)]
    out_specs = [pl.BlockSpec((tm, dm.d), row)]
    out_shape = [jax.ShapeDtypeStruct((dm.t, dm.d), F32)]
    if with_mod:
        in_specs += [_mod_spec(dm, tm, l + 1, 0), _mod_spec(dm, tm, l + 1, 1)]
        args += [mod4, mod4]
        out_specs.append(pl.BlockSpec((tm, dm.d), row))
        out_shape.append(jax.ShapeDtypeStruct((dm.t, dm.d), BF16))
    return pl.pallas_call(
        functools.partial(_ln_res_kernel, alpha=alpha, with_mod=with_mod),
        grid=(dm.t // tm,),
        in_specs=in_specs,
        out_specs=out_specs,
        out_shape=out_shape,
        compiler_params=_params(("arbitrary",)),
        name="ln2",
    )(*args)


def _ffn_up_kernel(e_ref, nu_ref, x_ref, wg_ref, wu_ref, h_ref, wgb_ref, wub_ref):
    i = pl.program_id(1)
    prev = e_ref[jnp.maximum(i - 1, 0)]

    @pl.when((i < nu_ref[0]) & ((i == 0) | (e_ref[i] != prev)))
    def _():
        wgb_ref[...] = wg_ref[...].astype(BF16)
        wub_ref[...] = wu_ref[...].astype(BF16)

    @pl.when(i < nu_ref[0])
    def _():
        x = x_ref[...]
        a = jnp.dot(x, wgb_ref[...], preferred_element_type=F32)
        u = jnp.dot(x, wub_ref[...], preferred_element_type=F32)
        h_ref[...] = (_silu(a) * u).astype(h_ref.dtype)

    @pl.when(i >= nu_ref[0])
    def _():
        h_ref[...] = jnp.zeros_like(h_ref)


def _ffn_down_kernel(e_ref, nu_ref, h_ref, wd_ref, y_ref, wdb_ref):
    i = pl.program_id(1)
    prev = e_ref[jnp.maximum(i - 1, 0)]

    @pl.when((i < nu_ref[0]) & ((i == 0) | (e_ref[i] != prev)))
    def _():
        wdb_ref[...] = wd_ref[...].astype(BF16)

    @pl.when(i < nu_ref[0])
    def _():
        y_ref[...] = jnp.dot(h_ref[...], wdb_ref[...], preferred_element_type=F32)

    @pl.when(i >= nu_ref[0])
    def _():
        y_ref[...] = jnp.zeros_like(y_ref)


def _expert_ffn(dm, xb, blk_e, n_used, w_gate, w_up, w_down, l):
    n_buf = xb.shape[0]
    n_blk = n_buf // MOE_ROWS
    tf = _tile(512, dm.dff)
    tn = _tile(2048, dm.d)

    def blk(i, nu):
        return jnp.minimum(i, nu[0] - 1)

    hid = pl.pallas_call(
        _ffn_up_kernel,
        grid_spec=pltpu.PrefetchScalarGridSpec(
            num_scalar_prefetch=2,
            grid=(dm.dff // tf, n_blk),
            in_specs=[pl.BlockSpec((MOE_ROWS, dm.d), lambda f, i, e, nu: (blk(i, nu), 0)),
                      pl.BlockSpec((None, None, dm.d, tf), lambda f, i, e, nu: (l, e[i], 0, f)),
                      pl.BlockSpec((None, None, dm.d, tf), lambda f, i, e, nu: (l, e[i], 0, f))],
            out_specs=pl.BlockSpec((MOE_ROWS, tf), lambda f, i, e, nu: (i, f)),
            scratch_shapes=[pltpu.VMEM((dm.d, tf), BF16), pltpu.VMEM((dm.d, tf), BF16)]),
        out_shape=jax.ShapeDtypeStruct((n_buf, dm.dff), BF16),
        compiler_params=_params(("arbitrary", "arbitrary")),
        name="ffn_up",
    )(blk_e, n_used, xb, w_gate, w_up)

    return pl.pallas_call(
        _ffn_down_kernel,
        grid_spec=pltpu.PrefetchScalarGridSpec(
            num_scalar_prefetch=2,
            grid=(dm.d // tn, n_blk),
            in_specs=[pl.BlockSpec((MOE_ROWS, dm.dff), lambda j, i, e, nu: (blk(i, nu), 0)),
                      pl.BlockSpec((None, None, dm.dff, tn), lambda j, i, e, nu: (l, e[i], 0, j))],
            out_specs=pl.BlockSpec((MOE_ROWS, tn), lambda j, i, e, nu: (i, j)),
            scratch_shapes=[pltpu.VMEM((dm.dff, tn), BF16)]),
        out_shape=jax.ShapeDtypeStruct((n_buf, dm.d), F32),
        compiler_params=_params(("arbitrary", "arbitrary")),
        name="ffn_down",
    )(blk_e, n_used, hid, w_down)


def _route_kernel(lg_ref, rb_ref, o_ref, cnt_ref, carry_ref, *, n_exp, gsz):
    @pl.when(pl.program_id(0) == 0)
    def _():
        carry_ref[...] = jnp.zeros_like(carry_ref)

    tm = lg_ref.shape[0]
    lane_i = lax.broadcasted_iota(jnp.int32, (tm, LANES), 1)
    lane = lane_i.astype(F32)
    valid = lane_i < n_exp
    pos = lane_i % gsz
    neg = -jnp.inf
    big = float(LANES)
    scores = _sigmoid(lg_ref[...])
    biased = jnp.where(valid, scores + rb_ref[...], neg)

    def member(s):
        fwd = pltpu.roll(biased, LANES - s, 1)
        back = pltpu.roll(biased, gsz - s, 1)
        return jnp.where(pos + s < gsz, fwd, back)

    xs = [biased] + [member(s) for s in range(1, gsz)]
    gscore = None
    for a in range(gsz):
        for b in range(a + 1, gsz):
            pair = xs[a] + xs[b]
            gscore = pair if gscore is None else jnp.maximum(gscore, pair)
    gscore = jnp.where(valid, gscore, neg)
    gmax = jnp.max(gscore, axis=1, keepdims=True)
    grp = (lane_i // gsz).astype(F32)
    sel = jnp.min(jnp.where(gscore == gmax, grp, big), axis=1, keepdims=True)
    m = jnp.where((grp == sel) & valid, biased, neg)
    m1 = jnp.max(m, axis=1, keepdims=True)
    i1 = jnp.min(jnp.where(m == m1, lane, big), axis=1, keepdims=True)
    m = jnp.where(lane == i1, neg, m)
    m2 = jnp.max(m, axis=1, keepdims=True)
    i2 = jnp.min(jnp.where(m == m2, lane, big), axis=1, keepdims=True)
    oh1 = lane == i1
    oh2 = lane == i2
    s1 = jnp.sum(jnp.where(oh1, scores, 0.0), axis=1, keepdims=True)
    s2 = jnp.sum(jnp.where(oh2, scores, 0.0), axis=1, keepdims=True)
    both = oh1.astype(F32) + oh2.astype(F32)
    ri = lax.broadcasted_iota(jnp.int32, (tm, tm), 0)
    ci = lax.broadcasted_iota(jnp.int32, (tm, tm), 1)
    earlier = (ri > ci).astype(BF16)
    tot = carry_ref[...] + jnp.dot(earlier, both.astype(BF16), preferred_element_type=F32)
    r1 = jnp.sum(jnp.where(oh1, tot, 0.0), axis=1, keepdims=True)
    r2 = jnp.sum(jnp.where(oh2, tot, 0.0), axis=1, keepdims=True)
    carry_ref[...] += jnp.sum(both, axis=0, keepdims=True)
    cnt_ref[...] = carry_ref[...]
    out = jnp.zeros((tm, LANES), F32)
    for k, val in enumerate((i1, i2, r1, r2, s1 / (s1 + s2), s2 / (s1 + s2))):
        out = jnp.where(lane_i == k, val, out)
    o_ref[...] = out


def _route(dm, logits, router_b):
    n_tok = dm.t
    tm = _tile(256, n_tok)
    ne = logits.shape[1]
    rb = jnp.pad(router_b.astype(F32), (0, ne - dm.n_exp)).reshape(1, ne)
    sel, cnt = pl.pallas_call(
        functools.partial(_route_kernel, n_exp=dm.n_exp, gsz=dm.n_exp // ROUTER_GROUPS),
        grid=(n_tok // tm,),
        in_specs=[pl.BlockSpec((tm, ne), lambda i: (i, 0)), pl.BlockSpec((1, ne), lambda i: (0, 0))],
        out_specs=[pl.BlockSpec((tm, ne), lambda i: (i, 0)), pl.BlockSpec((1, ne), lambda i: (0, 0))],
        out_shape=[jax.ShapeDtypeStruct((n_tok, ne), F32), jax.ShapeDtypeStruct((1, ne), F32)],
        scratch_shapes=[pltpu.VMEM((1, ne), F32)],
        compiler_params=_params(("arbitrary",)),
        name="route",
    )(logits, rb)
    top_e = sel[:, 0:TOP_K].astype(jnp.int32)
    rank = sel[:, TOP_K:2 * TOP_K].astype(jnp.int32)
    top_w = sel[:, 2 * TOP_K:3 * TOP_K]
    counts = cnt[0, :dm.n_exp].astype(jnp.int32)
    padded = (counts + MOE_ROWS - 1) // MOE_ROWS * MOE_ROWS
    pend = jnp.cumsum(padded)
    pstart = pend - padded
    slot_dest = pstart[top_e] + rank
    n_slot = n_tok * TOP_K
    n_buf = -(-n_slot // MOE_ROWS) * MOE_ROWS + dm.n_exp * MOE_ROWS
    n_blk = n_buf // MOE_ROWS
    tok = jnp.repeat(jnp.arange(n_tok, dtype=jnp.int32), TOP_K)
    buf_tok = jnp.full((n_buf,), n_tok, dtype=jnp.int32).at[slot_dest.reshape(n_slot)].set(tok)
    blk_e = jnp.minimum(jnp.searchsorted(pend, jnp.arange(n_blk, dtype=jnp.int32) * MOE_ROWS, side='right'),
                        dm.n_exp - 1).astype(jnp.int32)
    n_used = (pend[-1] // MOE_ROWS).astype(jnp.int32).reshape(1)
    return top_w, buf_tok, blk_e, slot_dest, n_used


def _moe(dm, h_bf, logits, router_b, w_gate, w_up, w_down, l):
    top_w, buf_tok, blk_e, slot_dest, n_used = _route(dm, logits, router_b)
    x_pad = jnp.concatenate([h_bf, jnp.zeros((1, dm.d), h_bf.dtype)], axis=0)
    xb = x_pad[buf_tok]
    yb = _expert_ffn(dm, xb, blk_e, n_used, w_gate, w_up, w_down, l)
    return jnp.sum(yb[slot_dest] * top_w[..., None], axis=1)


def _grid_pos_embed(rows, d):
    nf = d // 4
    omega = 1.0 / (POS_BASE ** (jnp.arange(nf, dtype=F32) / nf))
    ang_r = jnp.arange(rows, dtype=F32)[:, None] * omega
    ang_c = jnp.arange(GRID_W, dtype=F32)[:, None] * omega
    emb_r = jnp.concatenate([jnp.sin(ang_r), jnp.cos(ang_r)], -1)
    emb_c = jnp.concatenate([jnp.sin(ang_c), jnp.cos(ang_c)], -1)
    emb = jnp.concatenate([jnp.broadcast_to(emb_r[:, None], (rows, GRID_W, d // 2)),
                           jnp.broadcast_to(emb_c[None], (rows, GRID_W, d // 2))], -1)
    return emb.reshape(rows * GRID_W, d)


def _ssd_gate_layouts(dm, dt, a_log):
    g_n, hpg = SSM_GROUPS, dm.hpg
    dtr = dt.reshape(dm.t, 2, g_n, hpg).transpose(1, 2, 3, 0)
    alr = a_log.astype(F32).reshape(2, g_n, hpg, 1)
    return dtr, alr


def _gdn_gate_layouts(dm, g, beta, cl_list):
    j_n = dm.qkh
    g4 = g.reshape(dm.t, 2, j_n, 2).transpose(2, 0, 1, 3).reshape(j_n, dm.t, 4)
    b4 = beta.reshape(dm.t, 2, j_n, 2).transpose(2, 0, 1, 3).reshape(j_n, dm.t, 4)
    gb = jnp.concatenate([g4, b4], axis=-1)
    grows = []
    for (row_off, n_rows, cl) in cl_list:
        part = gb[:, row_off:row_off + n_rows]
        grows.append(part.reshape(j_n, n_rows // cl, cl, 8).transpose(0, 1, 3, 2))
    return grows


def kernel(x_prompt, x_sample, state_ssm, state_gdn, c, c_ctx, w_ada, b_ada, w_in, conv_ssm_w, conv_ssm_b, ssm_dt_bias, ssm_a_log, ssm_d, ssm_norm_w, conv_gdn_w, gdn_dt_bias, gdn_a_log, gdn_norm_w, w_branch_ssm, w_branch_gdn, w_out, ln1_g, ln1_b, ln2_g, ln2_b, router_w, router_b, w_gate, w_up, w_down):
    dm = _dims(x_prompt, x_sample, state_ssm, w_in, w_gate)
    alpha = (2.0 * dm.depth) ** 0.25

    cond = jnp.zeros((dm.ncond, dm.d), F32).at[0].set(c_ctx).at[1:1 + dm.db].set(c)
    mod4 = _ada_mod(dm, cond, w_ada, b_ada).reshape(dm.depth, dm.ncond, 1, 6 * dm.d)
    pos = _grid_pos_embed(dm.dseq // GRID_W, dm.d)
    x, h = _embed(dm, x_prompt.reshape(dm.tp, dm.d), x_sample.reshape(dm.t - dm.tp, dm.d), pos, mod4)

    rw_pad = jnp.pad(router_w, ((0, 0), (0, -router_w.shape[1] % LANES)))
    h0_ssm = state_ssm.reshape(dm.db, dm.depth, 2, SSM_GROUPS, dm.gw, dm.state)
    cl_ctx, cl_lat = min(SCAN_CHUNK, dm.seq), min(SCAN_CHUNK, dm.dseq)
    ssm_states, gdn_states = [], []
    for l in range(dm.depth):
        proj = _in_proj(dm, h, w_in, l)
        dt, g, beta = _gates(dm, proj[:, dm.off_dt:dm.off_dt + 2 * dm.heads],
                             proj[:, dm.off_a:dm.off_a + 2 * dm.vh], proj[:, dm.off_b:dm.off_b + 2 * dm.vh],
                             ssm_dt_bias[l], gdn_dt_bias[l], gdn_a_log[l])
        act_s = _conv_act(dm, proj, dm.off_xbc, dm.xbc, conv_ssm_w, conv_ssm_b, l)
        dtr, alr = _ssd_gate_layouts(dm, dt, ssm_a_log[l])
        dskip = jnp.repeat(ssm_d[l].astype(F32), SSM_HEADDIM).reshape(1, dm.inner)
        nw_s = ssm_norm_w[l].reshape(1, dm.inner)
        y_ssm, hs = _ssd_call(dm, act_s, proj, dtr, alr, dskip, nw_s, None, None, l, 0, dm.nb, dm.seq)
        (y_ssm,) = _ssd_call(dm, act_s, proj, dtr, alr, dskip, nw_s, y_ssm, h0_ssm, l, dm.tp, dm.db, dm.dseq)
        ssm_states.append(hs.reshape(dm.nb, 2, dm.heads, SSM_HEADDIM, dm.state))
        grow_ctx, grow_lat = _gdn_gate_layouts(dm, g, beta, [(0, dm.tp, cl_ctx), (dm.tp, dm.t - dm.tp, cl_lat)])
        nw_g = gdn_norm_w[l].reshape(1, GDN_DV)
        y_gdn, ss = _gdn_call(dm, proj, conv_gdn_w, grow_ctx, nw_g, None, None, l, 0, dm.nb, dm.seq)
        (y_gdn,) = _gdn_call(dm, proj, conv_gdn_w, grow_lat, nw_g, y_gdn, state_gdn, l, dm.tp, dm.db, dm.dseq)
        gdn_states.append(ss)
        merged = _branch_merge(dm, y_ssm, y_gdn, w_branch_ssm, w_branch_gdn, proj, l)
        pre = _out_proj(dm, merged, w_out, x, mod4, l, alpha)
        x, h2, logits = _ln_router(dm, pre, ln1_g, ln1_b, mod4, rw_pad, l)
        y_moe = _moe(dm, h2, logits, router_b, w_gate, w_up, w_down, l)
        outs = _ln_res(dm, x, y_moe, ln2_g, ln2_b, mod4, l, alpha)
        x = outs[0]
        if l + 1 < dm.depth:
            h = outs[1]

    y_prompt = x[:dm.tp].reshape(dm.nb, dm.seq, dm.d)
    y_sample = x[dm.tp:].reshape(dm.db, dm.dseq, dm.d)
    return (y_prompt, y_sample, jnp.stack(ssm_states, axis=1), jnp.stack(gdn_states, axis=1))
```

```python
import functools
import math
from typing import NamedTuple

import jax
import jax.numpy as jnp
from jax import lax
from jax.experimental import pallas as pl
from jax.experimental.pallas import tpu as pltpu

F32 = jnp.float32
BF16 = jnp.bfloat16
HI = lax.Precision.HIGHEST

CONV_K = 5
SSM_HEADDIM = 64
SSM_GROUPS = 8
GDN_DK = 128
GDN_DV = 128
ROUTER_GROUPS = 8
TOP_K = 2
GRID_W = 64
POS_BASE = 10000.0
LN_EPS = 1e-5
RMS_EPS = 1e-6

LANES = 128
SUBLANES = 8
VMEM_LIMIT = 56 * 1024 * 1024

SCAN_CHUNK = 128
SSD_HEADS_PER_STAGE = 4
MOE_ROWS = 128


class Dims(NamedTuple):
    d: int
    depth: int
    nb: int
    seq: int
    db: int
    dseq: int
    tp: int
    t: int
    inner: int
    heads: int
    hpg: int
    gw: int
    state: int
    xbc: int
    qkh: int
    vh: int
    qkv: int
    val: int
    n_exp: int
    dff: int
    in_cols: int
    off_xbc: int
    off_dt: int
    off_qkv: int
    off_zg: int
    off_a: int
    off_b: int
    off_ga: int
    off_gb: int
    ncond: int


def _dims(x_prompt, x_sample, state_ssm, w_in, w_gate):
    nb, seq, d = x_prompt.shape
    db, dseq, _ = x_sample.shape
    depth = w_in.shape[0]
    inner = 2 * d
    heads = inner // SSM_HEADDIM
    state = state_ssm.shape[-1]
    xbc = inner + 2 * SSM_GROUPS * state
    qkh = d // GDN_DK
    vh = 2 * qkh
    key_dim = qkh * GDN_DK
    val = vh * GDN_DV
    qkv = 2 * key_dim + val
    off_xbc = inner
    off_dt = off_xbc + xbc
    off_qkv = off_dt + 2 * heads
    off_zg = off_qkv + qkv
    off_a = off_zg + val
    off_b = off_a + 2 * vh
    off_ga = off_b + 2 * vh
    off_gb = off_ga + d
    in_cols = off_gb + d
    assert in_cols == w_in.shape[2]
    ncond = -(-(1 + db) // SUBLANES) * SUBLANES
    return Dims(d=d, depth=depth, nb=nb, seq=seq, db=db, dseq=dseq, tp=nb * seq, t=nb * seq + db * dseq,
                inner=inner, heads=heads, hpg=heads // SSM_GROUPS, gw=inner // SSM_GROUPS, state=state, xbc=xbc,
                qkh=qkh, vh=vh, qkv=qkv, val=val, n_exp=w_gate.shape[1], dff=w_gate.shape[3], in_cols=in_cols,
                off_xbc=off_xbc, off_dt=off_dt, off_qkv=off_qkv, off_zg=off_zg, off_a=off_a, off_b=off_b,
                off_ga=off_ga, off_gb=off_gb, ncond=ncond)


def _tile(pref, *sizes):
    g = 0
    for s in sizes:
        g = math.gcd(g, s)
    t = math.gcd(pref, g)
    return t


def _params(sem, vmem=VMEM_LIMIT):
    return pltpu.CompilerParams(dimension_semantics=sem, vmem_limit_bytes=vmem)


def _sigmoid(x):
    return 1.0 / (1.0 + jnp.exp(-x))


def _silu(x):
    return x * _sigmoid(x)


def _softplus(x):
    return jnp.maximum(x, 0.0) + jnp.log(1.0 + jnp.exp(-jnp.abs(x)))


def _cond_of_row(dm, row0):
    return jnp.where(row0 < dm.tp, 0, 1 + (row0 - dm.tp) // dm.dseq)


def _ada_kernel(c_ref, w_ref, b_ref, o_ref):
    s = _silu(c_ref[...]).astype(BF16)
    o_ref[...] = jnp.dot(s, w_ref[...].astype(BF16), preferred_element_type=F32) + b_ref[...]


def _ada_mod(dm, cond, w_ada, b_ada):
    n = w_ada.shape[2]
    tn = _tile(512, n)
    return pl.pallas_call(
        _ada_kernel,
        grid=(dm.depth, n // tn),
        in_specs=[pl.BlockSpec((dm.ncond, dm.d), lambda l, j: (0, 0)),
                  pl.BlockSpec((None, dm.d, tn), lambda l, j: (l, 0, j)),
                  pl.BlockSpec((None, 1, tn), lambda l, j: (l, 0, j))],
        out_specs=pl.BlockSpec((None, dm.ncond, tn), lambda l, j: (l, 0, j)),
        out_shape=jax.ShapeDtypeStruct((dm.depth, dm.ncond, n), F32),
        compiler_params=_params(("arbitrary", "arbitrary")),
        name="ada_mod",
    )(cond, w_ada, b_ada.reshape(dm.depth, 1, n))


def _mod_spec(dm, tm, l, which, width=None, col_of=None):
    width = dm.d if width is None else width
    per = dm.d // width

    def idx(*g):
        i, j = (g[0], 0) if col_of is None else col_of(*g)
        return (l, _cond_of_row(dm, i * tm), 0, which * per + j)

    return pl.BlockSpec((None, None, 1, width), idx)


def _embed_kernel(xp_ref, xs_ref, pos_ref, sh_ref, sc_ref, x_ref, h_ref, *, npb):
    is_ctx = pl.program_id(0) < npb
    x = jnp.where(is_ctx, xp_ref[...], xs_ref[...] + pos_ref[...])
    x_ref[...] = x
    h_ref[...] = (x * (1.0 + sc_ref[...]) + sh_ref[...]).astype(BF16)


def _embed(dm, xp, xs, pos, mod4):
    tm = _tile(256, dm.seq, dm.dseq)
    npb = dm.tp // tm
    nsb = (dm.t - dm.tp) // tm
    ppb = dm.dseq // tm
    row = lambda i: (i, 0)
    return pl.pallas_call(
        functools.partial(_embed_kernel, npb=npb),
        grid=(dm.t // tm,),
        in_specs=[pl.BlockSpec((tm, dm.d), lambda i: (jnp.minimum(i, npb - 1), 0)),
                  pl.BlockSpec((tm, dm.d), lambda i: (jnp.clip(i - npb, 0, nsb - 1), 0)),
                  pl.BlockSpec((tm, dm.d), lambda i: (jnp.maximum(i - npb, 0) % ppb, 0)),
                  _mod_spec(dm, tm, 0, 0), _mod_spec(dm, tm, 0, 1)],
        out_specs=[pl.BlockSpec((tm, dm.d), row), pl.BlockSpec((tm, dm.d), row)],
        out_shape=[jax.ShapeDtypeStruct((dm.t, dm.d), F32), jax.ShapeDtypeStruct((dm.t, dm.d), BF16)],
        compiler_params=_params(("arbitrary",)),
        name="embed_mod",
    )(xp, xs, pos, mod4, mod4)


def _mm_res_kernel(x_ref, w_ref, r_ref, g_ref, o_ref, wb_ref, *, alpha):
    @pl.when(pl.program_id(1) == 0)
    def _():
        wb_ref[...] = w_ref[...].astype(BF16)

    acc = jnp.dot(x_ref[...], wb_ref[...], preferred_element_type=F32)
    o_ref[...] = alpha * r_ref[...] + g_ref[...] * acc


def _mm_stream_kernel(x_ref, w_ref, o_ref):
    o_ref[...] = jnp.dot(x_ref[...], w_ref[...].astype(BF16), preferred_element_type=F32).astype(o_ref.dtype)


def _in_proj(dm, h, w_in, l):
    k = dm.d
    n = dm.in_cols
    tm = _tile(2048, dm.t)
    tn = _tile(256, n)
    return pl.pallas_call(
        _mm_stream_kernel,
        grid=(dm.t // tm, n // tn),
        in_specs=[pl.BlockSpec((tm, k), lambda i, j: (i, 0)),
                  pl.BlockSpec((None, k, tn), lambda i, j: (l, 0, j))],
        out_specs=pl.BlockSpec((tm, tn), lambda i, j: (i, j)),
        out_shape=jax.ShapeDtypeStruct((dm.t, n), F32),
        compiler_params=_params(("arbitrary", "arbitrary")),
        name="in_proj",
    )(h, w_in)


def _out_proj(dm, merged, w_out, x, mod4, l, alpha):
    k = dm.d
    tm = _tile(1024, dm.tp, dm.dseq)
    tn = _tile(512, dm.d)
    return pl.pallas_call(
        functools.partial(_mm_res_kernel, alpha=alpha),
        grid=(dm.d // tn, dm.t // tm),
        in_specs=[pl.BlockSpec((tm, k), lambda j, i: (i, 0)),
                  pl.BlockSpec((None, k, tn), lambda j, i: (l, 0, j)),
                  pl.BlockSpec((tm, tn), lambda j, i: (i, j)),
                  _mod_spec(dm, tm, l, 2, width=tn, col_of=lambda j, i: (i, j))],
        out_specs=pl.BlockSpec((tm, tn), lambda j, i: (i, j)),
        out_shape=jax.ShapeDtypeStruct((dm.t, dm.d), F32),
        scratch_shapes=[pltpu.VMEM((k, tn), BF16)],
        compiler_params=_params(("arbitrary", "arbitrary")),
        name="out_proj",
    )(merged, w_out, x, mod4)


def _branch_kernel(ya_ref, yb_ref, wa_ref, wb_ref, ga_ref, gb_ref, o_ref, acc_ref, res_ref, wpa_ref, wpb_ref,
                   *, nk, tk):
    i = pl.program_id(1)
    kb = pl.program_id(2)

    @pl.when(kb % nk == 0)
    def _():
        acc_ref[...] = jnp.zeros_like(acc_ref)

    def accumulate(y_ref, w_ref, wp_ref, kk):
        rows = pl.ds(pl.multiple_of(kk * tk, tk), tk)

        @pl.when(i == 0)
        def _():
            wp_ref[rows, :] = w_ref[...].astype(BF16)

        acc_ref[...] += jnp.dot(y_ref[...], wp_ref[rows, :], preferred_element_type=F32)

    @pl.when(kb < nk)
    def _():
        accumulate(ya_ref, wa_ref, wpa_ref, kb)

    @pl.when(kb >= nk)
    def _():
        accumulate(yb_ref, wb_ref, wpb_ref, kb - nk)

    @pl.when(kb == nk - 1)
    def _():
        res_ref[...] = _sigmoid(ga_ref[...]) * acc_ref[...]

    @pl.when(kb == 2 * nk - 1)
    def _():
        o_ref[...] = (res_ref[...] + _sigmoid(gb_ref[...]) * acc_ref[...]).astype(o_ref.dtype)


def _branch_merge(dm, y_ssm, y_gdn, w_a, w_b, proj, l):
    assert dm.inner == dm.val
    kdim = dm.inner
    tm = _tile(1024, dm.t)
    tn = _tile(512, dm.d, dm.off_ga, dm.off_gb)
    tk = _tile(1024, kdim)
    nk = kdim // tk
    ca, cb = dm.off_ga // tn, dm.off_gb // tn

    def wa_idx(j, i, kb):
        return (l, jnp.where(i == 0, jnp.minimum(kb, nk - 1), nk - 1), j)

    def wb_idx(j, i, kb):
        return (l, jnp.where(i == 0, jnp.maximum(kb - nk, 0), nk - 1), j)

    return pl.pallas_call(
        functools.partial(_branch_kernel, nk=nk, tk=tk),
        grid=(dm.d // tn, dm.t // tm, 2 * nk),
        in_specs=[pl.BlockSpec((tm, tk), lambda j, i, kb: (i, jnp.minimum(kb, nk - 1))),
                  pl.BlockSpec((tm, tk), lambda j, i, kb: (i, jnp.maximum(kb - nk, 0))),
                  pl.BlockSpec((None, tk, tn), wa_idx),
                  pl.BlockSpec((None, tk, tn), wb_idx),
                  pl.BlockSpec((tm, tn), lambda j, i, kb: (i, ca + j)),
                  pl.BlockSpec((tm, tn), lambda j, i, kb: (i, cb + j))],
        out_specs=pl.BlockSpec((tm, tn), lambda j, i, kb: (i, j)),
        out_shape=jax.ShapeDtypeStruct((dm.t, dm.d), BF16),
        scratch_shapes=[pltpu.VMEM((tm, tn), F32), pltpu.VMEM((tm, tn), F32),
                        pltpu.VMEM((kdim, tn), BF16), pltpu.VMEM((kdim, tn), BF16)],
        compiler_params=_params(("arbitrary", "arbitrary", "arbitrary")),
        name="branch_merge",
    )(y_ssm, y_gdn, w_a, w_b, proj, proj)


def _gates_kernel(dtr_ref, ar_ref, br_ref, dtb_ref, gb_ref, al_ref, dt_ref, g_ref, beta_ref):
    dt_ref[...] = _softplus(dtr_ref[...] + dtb_ref[...])
    g_ref[...] = -jnp.exp(al_ref[...]) * _softplus(ar_ref[...] + gb_ref[...])
    beta_ref[...] = _sigmoid(br_ref[...])


def _gates(dm, dt_raw, a_raw, b_raw, dt_bias, g_bias, a_log):
    tm = _tile(512, dm.t)
    w1, w2 = 2 * dm.heads, 2 * dm.vh
    row = lambda i: (i, 0)
    fix = lambda i: (0, 0)
    return pl.pallas_call(
        _gates_kernel,
        grid=(dm.t // tm,),
        in_specs=[pl.BlockSpec((tm, w1), row), pl.BlockSpec((tm, w2), row), pl.BlockSpec((tm, w2), row),
                  pl.BlockSpec((1, w1), fix), pl.BlockSpec((1, w2), fix), pl.BlockSpec((1, w2), fix)],
        out_specs=[pl.BlockSpec((tm, w1), row), pl.BlockSpec((tm, w2), row), pl.BlockSpec((tm, w2), row)],
        out_shape=[jax.ShapeDtypeStruct((dm.t, w1), F32), jax.ShapeDtypeStruct((dm.t, w2), F32),
                   jax.ShapeDtypeStruct((dm.t, w2), F32)],
        compiler_params=_params(("arbitrary",)),
        name="gate_prep",
    )(dt_raw, a_raw, b_raw, dt_bias.reshape(1, w1), g_bias.reshape(1, w2), a_log.reshape(1, w2))


def _conv_kernel(*refs, rows, blocks_ctx, blocks_per_ctx_seq, blocks_per_lat_seq, has_bias):
    if has_bias:
        x_ref, prev_ref, next_ref, w_ref, b_ref, o_ref = refs
    else:
        x_ref, prev_ref, next_ref, w_ref, o_ref = refs
        b_ref = None
    i = pl.program_id(0)
    in_ctx = i < blocks_ctx
    pos = jnp.where(in_ctx, i % blocks_per_ctx_seq, (i - blocks_ctx) % blocks_per_lat_seq)
    per = jnp.where(in_ctx, blocks_per_ctx_seq, blocks_per_lat_seq)
    prev = jnp.where(pos == 0, 0.0, prev_ref[...])
    nxt = jnp.where(pos == per - 1, 0.0, next_ref[...])
    ext = jnp.concatenate([prev, x_ref[...], nxt], axis=0)
    n = rows + 2 * SUBLANES
    pad = CONV_K // 2
    w = w_ref[...]
    acc = None
    for tap in range(CONV_K):
        start = SUBLANES - pad + tap
        shifted = ext if start == 0 else pltpu.roll(ext, n - start, 0)
        term = shifted[:rows] * w[tap:tap + 1, :]
        acc = term if acc is None else acc + term
    if has_bias:
        acc = acc + b_ref[...]
    o_ref[...] = _silu(acc)


def _conv_act(dm, proj, col_off, width, w, b, l):
    rows = _tile(256, dm.seq, dm.dseq)
    tc = _tile(2048, col_off, width)
    c0 = col_off // tc
    hb = rows // SUBLANES
    nhalo = dm.t // SUBLANES
    has_bias = b is not None
    in_specs = [pl.BlockSpec((rows, tc), lambda i, j: (i, c0 + j)),
                pl.BlockSpec((SUBLANES, tc), lambda i, j: (jnp.maximum(i * hb - 1, 0), c0 + j)),
                pl.BlockSpec((SUBLANES, tc), lambda i, j: (jnp.minimum((i + 1) * hb, nhalo - 1), c0 + j)),
                pl.BlockSpec((None, CONV_K, tc), lambda i, j: (l, 0, j))]
    args = [proj, proj, proj, w]
    if has_bias:
        in_specs.append(pl.BlockSpec((None, 1, tc), lambda i, j: (l, 0, j)))
        args.append(b.reshape(dm.depth, 1, width))
    return pl.pallas_call(
        functools.partial(_conv_kernel, rows=rows, blocks_ctx=dm.tp // rows, blocks_per_ctx_seq=dm.seq // rows,
                          blocks_per_lat_seq=dm.dseq // rows, has_bias=has_bias),
        grid=(dm.t // rows, width // tc),
        in_specs=in_specs,
        out_specs=pl.BlockSpec((rows, tc), lambda i, j: (i, j)),
        out_shape=jax.ShapeDtypeStruct((dm.t, width), F32),
        compiler_params=_params(("arbitrary", "arbitrary")),
        name="conv_act",
    )(*args)


def _ssd_kernel(*refs, nc, hpg, hd, has_state, want_state):
    it = iter(refs)
    xs_ref, b_ref, c_ref, dtr_ref, alr_ref, z_ref, dsk_ref, nw_ref = (next(it) for _ in range(8))
    h0_ref = next(it) if has_state else None
    y_ref = next(it)
    hout_ref = next(it) if want_state else None
    ht_ref, yf_ref, xw_ref, yc_ref = (next(it) for _ in range(4))

    d = pl.program_id(2)
    c = pl.program_id(3)
    q = xs_ref.shape[0]
    fwd = d == 0
    ii = lax.broadcasted_iota(jnp.int32, (q, q), 0)
    jj = lax.broadcasted_iota(jnp.int32, (q, q), 1)
    mask = ((ii - jj) * (1 - 2 * d)) >= 0
    tri = mask.astype(F32)

    dt_r = dtr_ref[...]
    a_r = dt_r * (-jnp.exp(alr_ref[...]))
    acum_r = lax.dot_general(a_r, tri, (((1,), (1,)), ((), ())), precision=HI, preferred_element_type=F32)
    alast_r = jnp.where(fwd, acum_r[:, q - 1:q], acum_r[:, 0:1])
    assert q == LANES and 2 * hpg <= LANES
    cols = jnp.concatenate([acum_r, dt_r, jnp.zeros((LANES - 2 * hpg, q), F32)], axis=0).T
    acum_c = cols[:, :hpg]
    dt_c = cols[:, hpg:2 * hpg]
    alast_c = jnp.where(fwd, acum_c[q - 1:q, :], acum_c[0:1, :])

    bm = b_ref[...].astype(BF16)
    cm = c_ref[...].astype(BF16)
    cb = lax.dot_general(cm, bm, (((1,), (1,)), ((), ())), preferred_element_type=F32)

    @pl.when(c == 0)
    def _():
        if has_state:
            ht_ref[...] = h0_ref[...].T
        else:
            ht_ref[...] = jnp.zeros_like(ht_ref)

    ht = ht_ref[...]
    yo = jnp.dot(cm, ht.astype(BF16), preferred_element_type=F32)
    xs = xs_ref[...]
    log_end = alast_c - acum_c + jnp.log(dt_c)
    for r0 in range(0, hpg, SSD_HEADS_PER_STAGE):
        heads = range(r0, min(r0 + SSD_HEADS_PER_STAGE, hpg))
        sls = [slice(r * hd, (r + 1) * hd) for r in heads]
        acs = [jnp.broadcast_to(acum_c[:, r:r + 1], (q, q)) for r in heads]
        les = [jnp.broadcast_to(log_end[:, r:r + 1], (q, hd)) for r in heads]
        decs = [jnp.where(mask, jnp.exp(ac - acum_r[r:r + 1, :]), 0.0) for r, ac in zip(heads, acs)]
        wgts = [(cb * dec * dt_r[r:r + 1, :]).astype(BF16) for r, dec in zip(heads, decs)]
        xrs = [xs[:, sl] for sl in sls]
        yds = [jnp.dot(wgt, x_r.astype(BF16), preferred_element_type=F32) for wgt, x_r in zip(wgts, xrs)]
        for sl, ac, le, x_r, yd in zip(sls, acs, les, xrs, yds):
            yc_ref[:, sl] = yd + yo[:, sl] * jnp.exp(ac[:, :hd])
            xw_ref[:, sl] = (x_r * jnp.exp(le)).astype(BF16)
    st = lax.dot_general(bm, xw_ref[...], (((0,), (0,)), ((), ())), preferred_element_type=F32)
    for r in range(hpg):
        sl = slice(r * hd, (r + 1) * hd)
        ht_ref[:, sl] = ht[:, sl] * jnp.exp(alast_r[r:r + 1, :]) + st[:, sl]

    cidx = c + d * (nc - 1 - 2 * c)
    rows = pl.ds(pl.multiple_of(cidx * q, q), q)

    @pl.when(fwd)
    def _():
        yf_ref[rows, :] = yc_ref[...]

    @pl.when(d == 1)
    def _():
        tot = yf_ref[rows, :] + yc_ref[...] + dsk_ref[...] * xs
        gated = tot * _silu(z_ref[...])
        ms = jnp.mean(gated * gated, axis=-1, keepdims=True)
        y_ref[...] = (gated * lax.rsqrt(ms + RMS_EPS) * nw_ref[...]).astype(y_ref.dtype)

    if want_state:
        @pl.when(c == nc - 1)
        def _():
            hout_ref[...] = ht_ref[...].T


def _ssd_call(dm, act, proj, dtr, alr, dskip, nw, y_prev, h0, l, row_off, nseq, length):
    q = min(SCAN_CHUNK, length)
    nc = length // q
    base = row_off // q
    g_n, gw, n = SSM_GROUPS, dm.gw, dm.state
    has_state = h0 is not None
    want_state = not has_state
    bcol = dm.inner // n

    def rb(b, d, c):
        return base + b * nc + c + d * (nc - 1 - 2 * c)

    def rb_out(b, d, c):
        return base + b * nc + jnp.where(d == 0, nc - 1, nc - 1 - c)

    in_specs = [pl.BlockSpec((q, gw), lambda b, g, d, c: (rb(b, d, c), g)),
                pl.BlockSpec((q, n), lambda b, g, d, c: (rb(b, d, c), bcol + g)),
                pl.BlockSpec((q, n), lambda b, g, d, c: (rb(b, d, c), bcol + g_n + g)),
                pl.BlockSpec((None, None, dm.hpg, q), lambda b, g, d, c: (d, g, 0, rb(b, d, c))),
                pl.BlockSpec((None, None, dm.hpg, 1), lambda b, g, d, c: (d, g, 0, 0)),
                pl.BlockSpec((q, gw), lambda b, g, d, c: (rb_out(b, d, c), g)),
                pl.BlockSpec((1, gw), lambda b, g, d, c: (0, g)),
                pl.BlockSpec((1, gw), lambda b, g, d, c: (0, g))]
    args = [act, act, act, dtr, alr, proj, dskip, nw]
    if has_state:
        in_specs.append(pl.BlockSpec((None, None, None, None, gw, n), lambda b, g, d, c: (b, l, d, g, 0, 0)))
        args.append(h0)
    out_specs = [pl.BlockSpec((q, gw), lambda b, g, d, c: (rb_out(b, d, c), g))]
    out_shape = [jax.ShapeDtypeStruct((dm.t, dm.inner), BF16)]
    if want_state:
        out_specs.append(pl.BlockSpec((None, None, None, gw, n), lambda b, g, d, c: (b, d, g, 0, 0)))
        out_shape.append(jax.ShapeDtypeStruct((nseq, 2, g_n, gw, n), F32))
    aliases = {}
    if y_prev is not None:
        in_specs.append(pl.BlockSpec(memory_space=pl.ANY))
        args.append(y_prev)
        aliases = {len(args) - 1: 0}

    def body(*refs):
        refs = list(refs)
        if y_prev is not None:
            n_in = len(args)
            del refs[n_in - 1]
        _ssd_kernel(*refs, nc=nc, hpg=dm.hpg, hd=SSM_HEADDIM, has_state=has_state, want_state=want_state)

    return pl.pallas_call(
        body,
        grid=(nseq, g_n, 2, nc),
        in_specs=in_specs,
        out_specs=out_specs,
        out_shape=out_shape,
        scratch_shapes=[pltpu.VMEM((n, gw), F32), pltpu.VMEM((length, gw), F32),
                        pltpu.VMEM((q, gw), BF16), pltpu.VMEM((q, gw), F32)],
        input_output_aliases=aliases,
        compiler_params=_params(("arbitrary",) * 4),
        name="ssd_scan",
    )(*args)


def _conv_silu_seq(x, w):
    n = x.shape[0]
    pad = CONV_K // 2
    t = lax.broadcasted_iota(jnp.int32, x.shape, 0)
    acc = x * w[pad:pad + 1, :]
    for tap in range(CONV_K):
        off = tap - pad
        if off == 0:
            continue
        shifted = pltpu.roll(x, (-off) % n, 0)
        inside = (t >= -off) if off < 0 else (t < n - off)
        acc = acc + jnp.where(inside, shifted, 0.0) * w[tap:tap + 1, :]
    return _silu(acc)


def _gdn_kernel(*refs, nc, cl, has_state, want_state):
    it = iter(refs)
    q_ref, k_ref, v0_ref, v1_ref, wq_ref, wk_ref, wv0_ref, wv1_ref = (next(it) for _ in range(8))
    z0_ref, z1_ref, grow_ref, nw_ref = (next(it) for _ in range(4))
    s0_ref = next(it) if has_state else None
    y_ref = next(it)
    sout_ref = next(it) if want_state else None
    qn_ref, kn_ref, va_ref, t_ref, qk_ref, gc_ref, o_ref, s_ref = (next(it) for _ in range(8))
    z_refs = (z0_ref, z1_ref)
    dk = q_ref.shape[1]
    dv = v0_ref.shape[1]

    qv = _conv_silu_seq(q_ref[...], wq_ref[...])
    qn_ref[...] = (qv * lax.rsqrt(jnp.sum(qv * qv, -1, keepdims=True) + RMS_EPS) * (dk ** -0.5)).astype(BF16)
    kv = _conv_silu_seq(k_ref[...], wk_ref[...])
    kn_ref[...] = (kv * lax.rsqrt(jnp.sum(kv * kv, -1, keepdims=True) + RMS_EPS)).astype(BF16)
    va_ref[0] = _conv_silu_seq(v0_ref[...], wv0_ref[...])
    va_ref[1] = _conv_silu_seq(v1_ref[...], wv1_ref[...])

    ii = lax.broadcasted_iota(jnp.int32, (cl, cl), 0)
    jj = lax.broadcasted_iota(jnp.int32, (cl, cl), 1)
    eye = (ii == jj).astype(F32)
    tri_f = (ii >= jj).astype(F32)
    incl = (ii >= jj, ii <= jj)
    strict = (ii > jj, ii < jj)
    assert cl == LANES
    sub = lax.broadcasted_iota(jnp.int32, (SUBLANES, cl), 0)
    n_lvl = int(math.log2(cl))
    pair_mask = [((ii >> (k + 1)) == (jj >> (k + 1))) & ((ii >> k) != (jj >> k)) for k in range(n_lvl)]
    nt = (((1,), (1,)), ((), ()))

    ua = 4 if nc % 4 == 0 else (2 if nc % 2 == 0 else 1)

    def phase_a(t, carry):
        nms, tms, slots = [], [], []
        for u in range(ua):
            ci = t * ua + u
            rows = pl.ds(pl.multiple_of(ci * cl, cl), cl)
            kc = kn_ref[rows, :]
            qc = qn_ref[rows, :]
            kk = lax.dot_general(kc, kc, nt, preferred_element_type=F32)
            qk = lax.dot_general(qc, kc, nt, preferred_element_type=F32)
            grow = grow_ref[ci]
            pre_r = lax.dot_general(grow, tri_f, nt, precision=HI, preferred_element_type=F32)
            suf_r = jnp.sum(grow, axis=1, keepdims=True) - pre_r + grow
            grb = jnp.where(sub >= 4, grow, jnp.where(sub >= 2, suf_r, pre_r))
            gcb = jnp.concatenate([grb, jnp.zeros((LANES - SUBLANES, cl), F32)], axis=0).T
            gc_ref[rows, :] = gcb
            for kx in range(4):
                dd = kx // 2
                gi = gcb[:, kx:kx + 1]
                gj = grb[kx:kx + 1, :]
                bi = gcb[:, 4 + kx:5 + kx]
                dec = jnp.where(incl[dd], jnp.exp(gi - gj), 0.0)
                nm = jnp.where(strict[dd], bi * kk * dec, 0.0)
                qk_ref[ci, kx] = (qk * dec).astype(BF16)
                nms.append(nm)
                tms.append(eye - jnp.where(pair_mask[0], nm, 0.0))
                slots.append((ci, kx))
        for lvl in range(1, n_lvl):
            tbs = [tm.astype(BF16) for tm in tms]
            cts = [jnp.dot(jnp.where(pair_mask[lvl], nm, 0.0).astype(BF16), tb, preferred_element_type=F32)
                   for nm, tb in zip(nms, tbs)]
            tms = [tm - jnp.dot(tb, ct.astype(BF16), preferred_element_type=F32)
                   for tm, tb, ct in zip(tms, tbs, cts)]
        for (ci, kx), tm in zip(slots, tms):
            t_ref[ci, kx] = tm.astype(BF16)
        return carry

    lax.fori_loop(0, nc // ua, phase_a, 0)

    for kx in range(4):
        if has_state:
            s_ref[kx] = s0_ref[kx // 2, kx % 2]
        else:
            s_ref[kx] = jnp.zeros((dk, dv), F32)

    def phase_b(t, carry):
        cidx = (t, t, nc - 1 - t, nc - 1 - t)
        rows = [pl.ds(pl.multiple_of(ci * cl, cl), cl) for ci in cidx]
        kcs = [kn_ref[r, :] for r in rows]
        ss = [s_ref[kx] for kx in range(4)]
        gis = [gc_ref[rows[kx], :][:, kx:kx + 1] for kx in range(4)]
        bis = [gc_ref[rows[kx], :][:, 4 + kx:5 + kx] for kx in range(4)]
        egis = [jnp.exp(gi) for gi in gis]
        kqs = [jnp.dot(jnp.concatenate([kcs[kx], qn_ref[rows[kx], :]], axis=0), ss[kx].astype(BF16),
                       preferred_element_type=F32) for kx in range(4)]
        rhs = [(bis[kx] * (va_ref[kx % 2, rows[kx], :] - egis[kx] * kqs[kx][:cl])).astype(BF16) for kx in range(4)]
        vnews = [jnp.dot(t_ref[cidx[kx], kx], rhs[kx], preferred_element_type=F32) for kx in range(4)]
        for kx in range(4):
            o_ref[kx // 2, kx % 2, rows[kx], :] = egis[kx] * kqs[kx][cl:] + jnp.dot(
                qk_ref[cidx[kx], kx], vnews[kx].astype(BF16), preferred_element_type=F32)
        glasts = [gis[kx][cl - 1:cl, :] if kx < 2 else gis[kx][0:1, :] for kx in range(4)]
        vss = [(jnp.exp(glasts[kx] - gis[kx]) * vnews[kx]).astype(BF16) for kx in range(4)]
        for kx in range(4):
            s_ref[kx] = ss[kx] * jnp.exp(glasts[kx]) + lax.dot_general(
                kcs[kx], vss[kx], (((0,), (0,)), ((), ())), preferred_element_type=F32)
        return carry

    lax.fori_loop(0, nc, phase_b, 0)

    for e in range(2):
        o = o_ref[0, e] + o_ref[1, e]
        on = o * lax.rsqrt(jnp.mean(o * o, -1, keepdims=True) + RMS_EPS)
        y_ref[:, e * dv:(e + 1) * dv] = (on * nw_ref[...] * _silu(z_refs[e][...])).astype(y_ref.dtype)
    if want_state:
        for kx in range(4):
            sout_ref[kx // 2, kx % 2] = s_ref[kx]


def _gdn_call(dm, proj, conv_w, grow, nw, y_prev, s0, l, row_off, nseq, length):
    cl = min(SCAN_CHUNK, length)
    nc = length // cl
    assert row_off % length == 0
    base = row_off // length
    dk, dv = GDN_DK, GDN_DV
    has_state = s0 is not None
    want_state = not has_state
    assert dm.off_qkv % dk == 0 and dm.off_zg % dv == 0
    qcol = dm.off_qkv // dk
    kcol = dm.qkh
    vcol = 2 * dm.qkh
    zcol = dm.off_zg // dv

    def col_specs(first, shape, row_of):
        return [pl.BlockSpec(shape, lambda b, j: row_of(b) + (first + j,)),
                pl.BlockSpec(shape, lambda b, j: row_of(b) + (first + kcol + j,)),
                pl.BlockSpec(shape, lambda b, j: row_of(b) + (first + vcol + 2 * j,)),
                pl.BlockSpec(shape, lambda b, j: row_of(b) + (first + vcol + 2 * j + 1,))]

    in_specs = (col_specs(qcol, (length, dk), lambda b: (base + b,))
                + col_specs(0, (None, CONV_K, dk), lambda b: (l, 0))
                + [pl.BlockSpec((length, dv), lambda b, j: (base + b, zcol + 2 * j)),
                   pl.BlockSpec((length, dv), lambda b, j: (base + b, zcol + 2 * j + 1)),
                   pl.BlockSpec((None, nc, SUBLANES, cl), lambda b, j: (j, b, 0, 0)),
                   pl.BlockSpec((1, dv), lambda b, j: (0, 0))])
    args = [proj] * 4 + [conv_w] * 4 + [proj, proj, grow, nw]
    if has_state:
        in_specs.append(pl.BlockSpec((None, None, 2, 2, dk, dv), lambda b, j: (b, l, 0, j, 0, 0)))
        args.append(s0)
    out_specs = [pl.BlockSpec((length, 2 * dv), lambda b, j: (base + b, j))]
    out_shape = [jax.ShapeDtypeStruct((dm.t, dm.val), BF16)]
    if want_state:
        out_specs.append(pl.BlockSpec((None, 2, 2, dk, dv), lambda b, j: (b, 0, j, 0, 0)))
        out_shape.append(jax.ShapeDtypeStruct((nseq, 2, dm.vh, dk, dv), F32))
    aliases = {}
    if y_prev is not None:
        in_specs.append(pl.BlockSpec(memory_space=pl.ANY))
        args.append(y_prev)
        aliases = {len(args) - 1: 0}

    def body(*refs):
        refs = list(refs)
        if y_prev is not None:
            del refs[len(args) - 1]
        _gdn_kernel(*refs, nc=nc, cl=cl, has_state=has_state, want_state=want_state)

    return pl.pallas_call(
        body,
        grid=(nseq, dm.qkh),
        in_specs=in_specs,
        out_specs=out_specs,
        out_shape=out_shape,
        scratch_shapes=[pltpu.VMEM((length, dk), BF16), pltpu.VMEM((length, dk), BF16),
                        pltpu.VMEM((2, length, dv), F32),
                        pltpu.VMEM((nc, 4, cl, cl), BF16), pltpu.VMEM((nc, 4, cl, cl), BF16),
                        pltpu.VMEM((length, LANES), F32),
                        pltpu.VMEM((2, 2, length, dv), F32), pltpu.VMEM((4, dk, dv), F32)],
        input_output_aliases=aliases,
        compiler_params=_params(("arbitrary", "arbitrary")),
        name="gdn_scan",
    )(*args)


def _layer_norm(x, g, b):
    mu = jnp.mean(x, -1, keepdims=True)
    xc = x - mu
    var = jnp.mean(xc * xc, -1, keepdims=True)
    return xc * lax.rsqrt(var + LN_EPS) * g + b


def _ln_router_kernel(pre_ref, g_ref, b_ref, sh_ref, sc_ref, rw_ref, x_ref, h_ref, lg_ref):
    x = _layer_norm(pre_ref[...], g_ref[...], b_ref[...])
    x_ref[...] = x
    h = x * (1.0 + sc_ref[...]) + sh_ref[...]
    h_ref[...] = h
    lg_ref[...] = jnp.dot(h, rw_ref[...], precision=HI, preferred_element_type=F32)


def _ln_router(dm, pre, g, b, mod4, rw_pad, l):
    tm = _tile(256, dm.seq, dm.dseq)
    row = lambda i: (i, 0)
    fix = lambda i: (0, 0)
    ne = rw_pad.shape[1]
    return pl.pallas_call(
        _ln_router_kernel,
        grid=(dm.t // tm,),
        in_specs=[pl.BlockSpec((tm, dm.d), row),
                  pl.BlockSpec((None, 1, dm.d), lambda i: (l, 0, 0)),
                  pl.BlockSpec((None, 1, dm.d), lambda i: (l, 0, 0)),
                  _mod_spec(dm, tm, l, 3), _mod_spec(dm, tm, l, 4),
                  pl.BlockSpec((dm.d, ne), fix)],
        out_specs=[pl.BlockSpec((tm, dm.d), row), pl.BlockSpec((tm, dm.d), row), pl.BlockSpec((tm, ne), row)],
        out_shape=[jax.ShapeDtypeStruct((dm.t, dm.d), F32), jax.ShapeDtypeStruct((dm.t, dm.d), F32),
                   jax.ShapeDtypeStruct((dm.t, ne), F32)],
        compiler_params=_params(("arbitrary",)),
        name="ln1_router",
    )(pre, g.reshape(dm.depth, 1, dm.d), b.reshape(dm.depth, 1, dm.d), mod4, mod4, rw_pad)


def _ln_res_kernel(*refs, alpha, with_mod, tm):
    if with_mod:
        slot_ref, x_ref, yb_ref, sel_ref, gt_ref, g_ref, b_ref, sh_ref, sc_ref, o_ref, h_ref, rows_ref, sem = refs
    else:
        slot_ref, x_ref, yb_ref, sel_ref, gt_ref, g_ref, b_ref, o_ref, rows_ref, sem = refs
    base = pl.program_id(0) * tm

    def copy(s):
        tok, k = s // TOP_K, s % TOP_K
        return _row_copy(yb_ref, slot_ref[(base + tok) * TOP_K + k], rows_ref.at[k], tok, sem)

    def start(s, carry):
        copy(s).start()
        return carry

    def wait(s, carry):
        copy(s).wait()
        return carry

    lax.fori_loop(0, tm * TOP_K, start, 0)
    lax.fori_loop(0, tm * TOP_K, wait, 0)
    sel = sel_ref[...]
    y = rows_ref[0] * sel[:, 2 * TOP_K:2 * TOP_K + 1]
    for k in range(1, TOP_K):
        y = y + rows_ref[k] * sel[:, 2 * TOP_K + k:2 * TOP_K + k + 1]
    x = _layer_norm(alpha * x_ref[...] + gt_ref[...] * y, g_ref[...], b_ref[...])
    o_ref[...] = x
    if with_mod:
        h_ref[...] = (x * (1.0 + sc_ref[...]) + sh_ref[...]).astype(BF16)


def _ln_res(dm, x, yb, slot_dest, sel, g, b, mod4, l, alpha):
    tm = _tile(256, dm.seq, dm.dseq)
    row = lambda i, s: (i, 0)
    with_mod = l + 1 < dm.depth
    ne = sel.shape[1]

    def mod_spec(layer, which):
        inner = _mod_spec(dm, tm, layer, which)
        return pl.BlockSpec(inner.block_shape, lambda i, s: inner.index_map(i))

    in_specs = [pl.BlockSpec((tm, dm.d), row), pl.BlockSpec(memory_space=pl.ANY), pl.BlockSpec((tm, ne), row),
                mod_spec(l, 5),
                pl.BlockSpec((None, 1, dm.d), lambda i, s: (l, 0, 0)),
                pl.BlockSpec((None, 1, dm.d), lambda i, s: (l, 0, 0))]
    args = [x, yb, sel, mod4, g.reshape(dm.depth, 1, dm.d), b.reshape(dm.depth, 1, dm.d)]
    out_specs = [pl.BlockSpec((tm, dm.d), row)]
    out_shape = [jax.ShapeDtypeStruct((dm.t, dm.d), F32)]
    if with_mod:
        in_specs += [mod_spec(l + 1, 0), mod_spec(l + 1, 1)]
        args += [mod4, mod4]
        out_specs.append(pl.BlockSpec((tm, dm.d), row))
        out_shape.append(jax.ShapeDtypeStruct((dm.t, dm.d), BF16))
    return pl.pallas_call(
        functools.partial(_ln_res_kernel, alpha=alpha, with_mod=with_mod, tm=tm),
        grid_spec=pltpu.PrefetchScalarGridSpec(
            num_scalar_prefetch=1,
            grid=(dm.t // tm,),
            in_specs=in_specs,
            out_specs=out_specs,
            scratch_shapes=[pltpu.VMEM((TOP_K, tm, dm.d), F32), pltpu.SemaphoreType.DMA(())]),
        out_shape=out_shape,
        compiler_params=_params(("arbitrary",)),
        name="ln2",
    )(slot_dest, *args)


def _ffn_up_kernel(e_ref, nu_ref, x_ref, wg_ref, wu_ref, h_ref, wgb_ref, wub_ref):
    i = pl.program_id(1)
    prev = e_ref[jnp.maximum(i - 1, 0)]

    @pl.when((i < nu_ref[0]) & ((i == 0) | (e_ref[i] != prev)))
    def _():
        wgb_ref[...] = wg_ref[...].astype(BF16)
        wub_ref[...] = wu_ref[...].astype(BF16)

    @pl.when(i < nu_ref[0])
    def _():
        x = x_ref[...].astype(BF16)
        a = jnp.dot(x, wgb_ref[...], preferred_element_type=F32)
        u = jnp.dot(x, wub_ref[...], preferred_element_type=F32)
        h_ref[...] = (_silu(a) * u).astype(h_ref.dtype)

    @pl.when(i >= nu_ref[0])
    def _():
        h_ref[...] = jnp.zeros_like(h_ref)


def _ffn_down_kernel(e_ref, nu_ref, h_ref, wd_ref, y_ref, wdb_ref):
    i = pl.program_id(1)
    prev = e_ref[jnp.maximum(i - 1, 0)]

    @pl.when((i < nu_ref[0]) & ((i == 0) | (e_ref[i] != prev)))
    def _():
        wdb_ref[...] = wd_ref[...].astype(BF16)

    @pl.when(i < nu_ref[0])
    def _():
        y_ref[...] = jnp.dot(h_ref[...], wdb_ref[...], preferred_element_type=F32)

    @pl.when(i >= nu_ref[0])
    def _():
        y_ref[...] = jnp.zeros_like(y_ref)


def _expert_ffn(dm, xb, blk_e, n_used, w_gate, w_up, w_down, l):
    n_buf = xb.shape[0]
    n_blk = n_buf // MOE_ROWS
    tf = _tile(512, dm.dff)
    tn = _tile(2048, dm.d)

    def blk(i, nu):
        return jnp.minimum(i, nu[0] - 1)

    hid = pl.pallas_call(
        _ffn_up_kernel,
        grid_spec=pltpu.PrefetchScalarGridSpec(
            num_scalar_prefetch=2,
            grid=(dm.dff // tf, n_blk),
            in_specs=[pl.BlockSpec((MOE_ROWS, dm.d), lambda f, i, e, nu: (blk(i, nu), 0)),
                      pl.BlockSpec((None, None, dm.d, tf), lambda f, i, e, nu: (l, e[i], 0, f)),
                      pl.BlockSpec((None, None, dm.d, tf), lambda f, i, e, nu: (l, e[i], 0, f))],
            out_specs=pl.BlockSpec((MOE_ROWS, tf), lambda f, i, e, nu: (i, f)),
            scratch_shapes=[pltpu.VMEM((dm.d, tf), BF16), pltpu.VMEM((dm.d, tf), BF16)]),
        out_shape=jax.ShapeDtypeStruct((n_buf, dm.dff), BF16),
        compiler_params=_params(("arbitrary", "arbitrary")),
        name="ffn_up",
    )(blk_e, n_used, xb, w_gate, w_up)

    return pl.pallas_call(
        _ffn_down_kernel,
        grid_spec=pltpu.PrefetchScalarGridSpec(
            num_scalar_prefetch=2,
            grid=(dm.d // tn, n_blk),
            in_specs=[pl.BlockSpec((MOE_ROWS, dm.dff), lambda j, i, e, nu: (blk(i, nu), 0)),
                      pl.BlockSpec((None, None, dm.dff, tn), lambda j, i, e, nu: (l, e[i], 0, j))],
            out_specs=pl.BlockSpec((MOE_ROWS, tn), lambda j, i, e, nu: (i, j)),
            scratch_shapes=[pltpu.VMEM((dm.dff, tn), BF16)]),
        out_shape=jax.ShapeDtypeStruct((n_buf, dm.d), F32),
        compiler_params=_params(("arbitrary", "arbitrary")),
        name="ffn_down",
    )(blk_e, n_used, hid, w_down)


def _route_kernel(lg_ref, rb_ref, o_ref, cnt_ref, carry_ref, *, n_exp, gsz):
    @pl.when(pl.program_id(0) == 0)
    def _():
        carry_ref[...] = jnp.zeros_like(carry_ref)

    tm = lg_ref.shape[0]
    lane_i = lax.broadcasted_iota(jnp.int32, (tm, LANES), 1)
    lane = lane_i.astype(F32)
    valid = lane_i < n_exp
    pos = lane_i % gsz
    neg = -jnp.inf
    big = float(LANES)
    scores = _sigmoid(lg_ref[...])
    biased = jnp.where(valid, scores + rb_ref[...], neg)

    def member(s):
        fwd = pltpu.roll(biased, LANES - s, 1)
        back = pltpu.roll(biased, gsz - s, 1)
        return jnp.where(pos + s < gsz, fwd, back)

    xs = [biased] + [member(s) for s in range(1, gsz)]
    gscore = None
    for a in range(gsz):
        for b in range(a + 1, gsz):
            pair = xs[a] + xs[b]
            gscore = pair if gscore is None else jnp.maximum(gscore, pair)
    gscore = jnp.where(valid, gscore, neg)
    gmax = jnp.max(gscore, axis=1, keepdims=True)
    grp = (lane_i // gsz).astype(F32)
    sel = jnp.min(jnp.where(gscore == gmax, grp, big), axis=1, keepdims=True)
    m = jnp.where((grp == sel) & valid, biased, neg)
    m1 = jnp.max(m, axis=1, keepdims=True)
    i1 = jnp.min(jnp.where(m == m1, lane, big), axis=1, keepdims=True)
    m = jnp.where(lane == i1, neg, m)
    m2 = jnp.max(m, axis=1, keepdims=True)
    i2 = jnp.min(jnp.where(m == m2, lane, big), axis=1, keepdims=True)
    oh1 = lane == i1
    oh2 = lane == i2
    s1 = jnp.sum(jnp.where(oh1, scores, 0.0), axis=1, keepdims=True)
    s2 = jnp.sum(jnp.where(oh2, scores, 0.0), axis=1, keepdims=True)
    both = oh1.astype(F32) + oh2.astype(F32)
    ri = lax.broadcasted_iota(jnp.int32, (tm, tm), 0)
    ci = lax.broadcasted_iota(jnp.int32, (tm, tm), 1)
    earlier = (ri > ci).astype(BF16)
    tot = carry_ref[...] + jnp.dot(earlier, both.astype(BF16), preferred_element_type=F32)
    r1 = jnp.sum(jnp.where(oh1, tot, 0.0), axis=1, keepdims=True)
    r2 = jnp.sum(jnp.where(oh2, tot, 0.0), axis=1, keepdims=True)
    carry_ref[...] += jnp.sum(both, axis=0, keepdims=True)
    cnt_ref[...] = carry_ref[...]
    out = jnp.zeros((tm, LANES), F32)
    for k, val in enumerate((i1, i2, r1, r2, s1 / (s1 + s2), s2 / (s1 + s2))):
        out = jnp.where(lane_i == k, val, out)
    o_ref[...] = out


def _route(dm, logits, router_b):
    n_tok = dm.t
    tm = _tile(256, n_tok)
    ne = logits.shape[1]
    rb = jnp.pad(router_b.astype(F32), (0, ne - dm.n_exp)).reshape(1, ne)
    sel, cnt = pl.pallas_call(
        functools.partial(_route_kernel, n_exp=dm.n_exp, gsz=dm.n_exp // ROUTER_GROUPS),
        grid=(n_tok // tm,),
        in_specs=[pl.BlockSpec((tm, ne), lambda i: (i, 0)), pl.BlockSpec((1, ne), lambda i: (0, 0))],
        out_specs=[pl.BlockSpec((tm, ne), lambda i: (i, 0)), pl.BlockSpec((1, ne), lambda i: (0, 0))],
        out_shape=[jax.ShapeDtypeStruct((n_tok, ne), F32), jax.ShapeDtypeStruct((1, ne), F32)],
        scratch_shapes=[pltpu.VMEM((1, ne), F32)],
        compiler_params=_params(("arbitrary",)),
        name="route",
    )(logits, rb)
    top_e = sel[:, 0:TOP_K].astype(jnp.int32)
    rank = sel[:, TOP_K:2 * TOP_K].astype(jnp.int32)
    counts = cnt[0, :dm.n_exp].astype(jnp.int32)
    padded = (counts + MOE_ROWS - 1) // MOE_ROWS * MOE_ROWS
    pend = jnp.cumsum(padded)
    pstart = pend - padded
    slot_dest = pstart[top_e] + rank
    n_slot = n_tok * TOP_K
    n_buf = -(-n_slot // MOE_ROWS) * MOE_ROWS + dm.n_exp * MOE_ROWS
    n_blk = n_buf // MOE_ROWS
    tok = jnp.repeat(jnp.arange(n_tok, dtype=jnp.int32), TOP_K)
    buf_tok = jnp.full((n_buf,), n_tok, dtype=jnp.int32).at[slot_dest.reshape(n_slot)].set(tok)
    blk_e = jnp.minimum(jnp.searchsorted(pend, jnp.arange(n_blk, dtype=jnp.int32) * MOE_ROWS, side='right'),
                        dm.n_exp - 1).astype(jnp.int32)
    n_used = (pend[-1] // MOE_ROWS).astype(jnp.int32).reshape(1)
    return sel, buf_tok, blk_e, slot_dest, n_used


def _row_copy(src_ref, src_row, dst_ref, dst_row, sem):
    return pltpu.make_async_copy(src_ref.at[pl.ds(src_row, 1)], dst_ref.at[pl.ds(dst_row, 1)], sem)


def _gather_kernel(tok_ref, h_ref, z_ref, xb_ref, sem, *, rows, n_tok):
    base = pl.program_id(0) * rows

    def copy(r):
        tok = tok_ref[base + r]
        real = _row_copy(h_ref, jnp.minimum(tok, n_tok - 1), xb_ref, base + r, sem)
        fill = _row_copy(z_ref, 0, xb_ref, base + r, sem)
        return tok < n_tok, real, fill

    def start(r, carry):
        is_real, real, fill = copy(r)

        @pl.when(is_real)
        def _():
            real.start()

        @pl.when(jnp.logical_not(is_real))
        def _():
            fill.start()

        return carry

    def wait(r, carry):
        is_real, real, fill = copy(r)

        @pl.when(is_real)
        def _():
            real.wait()

        @pl.when(jnp.logical_not(is_real))
        def _():
            fill.wait()

        return carry

    lax.fori_loop(0, rows, start, 0)
    lax.fori_loop(0, rows, wait, 0)


def _gather_rows(dm, h, buf_tok):
    n_buf = buf_tok.shape[0]
    zero = jnp.zeros((SUBLANES, dm.d), h.dtype)
    return pl.pallas_call(
        functools.partial(_gather_kernel, rows=MOE_ROWS, n_tok=dm.t),
        grid_spec=pltpu.PrefetchScalarGridSpec(
            num_scalar_prefetch=1,
            grid=(n_buf // MOE_ROWS,),
            in_specs=[pl.BlockSpec(memory_space=pl.ANY), pl.BlockSpec(memory_space=pl.ANY)],
            out_specs=pl.BlockSpec(memory_space=pl.ANY),
            scratch_shapes=[pltpu.SemaphoreType.DMA(())]),
        out_shape=jax.ShapeDtypeStruct((n_buf, dm.d), h.dtype),
        compiler_params=_params(("arbitrary",)),
        name="moe_gather",
    )(buf_tok, h, zero)


def _moe(dm, h, logits, router_b, w_gate, w_up, w_down, l):
    sel, buf_tok, blk_e, slot_dest, n_used = _route(dm, logits, router_b)
    xb = _gather_rows(dm, h, buf_tok)
    yb = _expert_ffn(dm, xb, blk_e, n_used, w_gate, w_up, w_down, l)
    return yb, slot_dest.reshape(dm.t * TOP_K), sel


def _grid_pos_embed(rows, d):
    nf = d // 4
    omega = 1.0 / (POS_BASE ** (jnp.arange(nf, dtype=F32) / nf))
    ang_r = jnp.arange(rows, dtype=F32)[:, None] * omega
    ang_c = jnp.arange(GRID_W, dtype=F32)[:, None] * omega
    emb_r = jnp.concatenate([jnp.sin(ang_r), jnp.cos(ang_r)], -1)
    emb_c = jnp.concatenate([jnp.sin(ang_c), jnp.cos(ang_c)], -1)
    emb = jnp.concatenate([jnp.broadcast_to(emb_r[:, None], (rows, GRID_W, d // 2)),
                           jnp.broadcast_to(emb_c[None], (rows, GRID_W, d // 2))], -1)
    return emb.reshape(rows * GRID_W, d)


def _ssd_gate_layouts(dm, dt, a_log):
    g_n, hpg = SSM_GROUPS, dm.hpg
    dtr = dt.reshape(dm.t, 2, g_n, hpg).transpose(1, 2, 3, 0)
    alr = a_log.astype(F32).reshape(2, g_n, hpg, 1)
    return dtr, alr


def _gdn_gate_layouts(dm, g, beta, cl_list):
    j_n = dm.qkh
    g4 = g.reshape(dm.t, 2, j_n, 2).transpose(2, 0, 1, 3).reshape(j_n, dm.t, 4)
    b4 = beta.reshape(dm.t, 2, j_n, 2).transpose(2, 0, 1, 3).reshape(j_n, dm.t, 4)
    gb = jnp.concatenate([g4, b4], axis=-1)
    grows = []
    for (row_off, n_rows, cl) in cl_list:
        part = gb[:, row_off:row_off + n_rows]
        grows.append(part.reshape(j_n, n_rows // cl, cl, 8).transpose(0, 1, 3, 2))
    return grows


def kernel(x_prompt, x_sample, state_ssm, state_gdn, c, c_ctx, w_ada, b_ada, w_in, conv_ssm_w, conv_ssm_b, ssm_dt_bias, ssm_a_log, ssm_d, ssm_norm_w, conv_gdn_w, gdn_dt_bias, gdn_a_log, gdn_norm_w, w_branch_ssm, w_branch_gdn, w_out, ln1_g, ln1_b, ln2_g, ln2_b, router_w, router_b, w_gate, w_up, w_down):
    dm = _dims(x_prompt, x_sample, state_ssm, w_in, w_gate)
    alpha = (2.0 * dm.depth) ** 0.25

    cond = jnp.zeros((dm.ncond, dm.d), F32).at[0].set(c_ctx).at[1:1 + dm.db].set(c)
    mod4 = _ada_mod(dm, cond, w_ada, b_ada).reshape(dm.depth, dm.ncond, 1, 6 * dm.d)
    pos = _grid_pos_embed(dm.dseq // GRID_W, dm.d)
    x, h = _embed(dm, x_prompt.reshape(dm.tp, dm.d), x_sample.reshape(dm.t - dm.tp, dm.d), pos, mod4)

    rw_pad = jnp.pad(router_w, ((0, 0), (0, -router_w.shape[1] % LANES)))
    h0_ssm = state_ssm.reshape(dm.db, dm.depth, 2, SSM_GROUPS, dm.gw, dm.state)
    cl_ctx, cl_lat = min(SCAN_CHUNK, dm.seq), min(SCAN_CHUNK, dm.dseq)
    ssm_states, gdn_states = [], []
    for l in range(dm.depth):
        proj = _in_proj(dm, h, w_in, l)
        dt, g, beta = _gates(dm, proj[:, dm.off_dt:dm.off_dt + 2 * dm.heads],
                             proj[:, dm.off_a:dm.off_a + 2 * dm.vh], proj[:, dm.off_b:dm.off_b + 2 * dm.vh],
                             ssm_dt_bias[l], gdn_dt_bias[l], gdn_a_log[l])
        act_s = _conv_act(dm, proj, dm.off_xbc, dm.xbc, conv_ssm_w, conv_ssm_b, l)
        dtr, alr = _ssd_gate_layouts(dm, dt, ssm_a_log[l])
        dskip = jnp.repeat(ssm_d[l].astype(F32), SSM_HEADDIM).reshape(1, dm.inner)
        nw_s = ssm_norm_w[l].reshape(1, dm.inner)
        y_ssm, hs = _ssd_call(dm, act_s, proj, dtr, alr, dskip, nw_s, None, None, l, 0, dm.nb, dm.seq)
        (y_ssm,) = _ssd_call(dm, act_s, proj, dtr, alr, dskip, nw_s, y_ssm, h0_ssm, l, dm.tp, dm.db, dm.dseq)
        ssm_states.append(hs.reshape(dm.nb, 2, dm.heads, SSM_HEADDIM, dm.state))
        grow_ctx, grow_lat = _gdn_gate_layouts(dm, g, beta, [(0, dm.tp, cl_ctx), (dm.tp, dm.t - dm.tp, cl_lat)])
        nw_g = gdn_norm_w[l].reshape(1, GDN_DV)
        y_gdn, ss = _gdn_call(dm, proj, conv_gdn_w, grow_ctx, nw_g, None, None, l, 0, dm.nb, dm.seq)
        (y_gdn,) = _gdn_call(dm, proj, conv_gdn_w, grow_lat, nw_g, y_gdn, state_gdn, l, dm.tp, dm.db, dm.dseq)
        gdn_states.append(ss)
        merged = _branch_merge(dm, y_ssm, y_gdn, w_branch_ssm, w_branch_gdn, proj, l)
        pre = _out_proj(dm, merged, w_out, x, mod4, l, alpha)
        x, h2, logits = _ln_router(dm, pre, ln1_g, ln1_b, mod4, rw_pad, l)
        yb, slot_dest, sel = _moe(dm, h2, logits, router_b, w_gate, w_up, w_down, l)
        outs = _ln_res(dm, x, yb, slot_dest, sel, ln2_g, ln2_b, mod4, l, alpha)
        x = outs[0]
        if l + 1 < dm.depth:
            h = outs[1]

    y_prompt = x[:dm.tp].reshape(dm.nb, dm.seq, dm.d)
    y_sample = x[dm.tp:].reshape(dm.db, dm.dseq, dm.d)
    return (y_prompt, y_sample, jnp.stack(ssm_states, axis=1), jnp.stack(gdn_states, axis=1))
```

```python
import functools
import math
from typing import NamedTuple

import jax
import jax.numpy as jnp
from jax import lax
from jax.experimental import pallas as pl
from jax.experimental.pallas import tpu as pltpu

F32 = jnp.float32
BF16 = jnp.bfloat16
HI = lax.Precision.HIGHEST

CONV_K = 5
SSM_HEADDIM = 64
SSM_GROUPS = 8
GDN_DK = 128
GDN_DV = 128
ROUTER_GROUPS = 8
TOP_K = 2
GRID_W = 64
POS_BASE = 10000.0
LN_EPS = 1e-5
RMS_EPS = 1e-6

LANES = 128
SUBLANES = 8
VMEM_LIMIT = 56 * 1024 * 1024

SCAN_CHUNK = 128
SSD_HEADS_PER_STAGE = 4
MOE_ROWS = 128


class Dims(NamedTuple):
    d: int
    depth: int
    nb: int
    seq: int
    db: int
    dseq: int
    tp: int
    t: int
    inner: int
    heads: int
    hpg: int
    gw: int
    state: int
    xbc: int
    qkh: int
    vh: int
    qkv: int
    val: int
    n_exp: int
    dff: int
    in_cols: int
    off_xbc: int
    off_dt: int
    off_qkv: int
    off_zg: int
    off_a: int
    off_b: int
    off_ga: int
    off_gb: int
    ncond: int


def _dims(x_prompt, x_sample, state_ssm, w_in, w_gate):
    nb, seq, d = x_prompt.shape
    db, dseq, _ = x_sample.shape
    depth = w_in.shape[0]
    inner = 2 * d
    heads = inner // SSM_HEADDIM
    state = state_ssm.shape[-1]
    xbc = inner + 2 * SSM_GROUPS * state
    qkh = d // GDN_DK
    vh = 2 * qkh
    key_dim = qkh * GDN_DK
    val = vh * GDN_DV
    qkv = 2 * key_dim + val
    off_xbc = inner
    off_dt = off_xbc + xbc
    off_qkv = off_dt + 2 * heads
    off_zg = off_qkv + qkv
    off_a = off_zg + val
    off_b = off_a + 2 * vh
    off_ga = off_b + 2 * vh
    off_gb = off_ga + d
    in_cols = off_gb + d
    assert in_cols == w_in.shape[2]
    ncond = -(-(1 + db) // SUBLANES) * SUBLANES
    return Dims(d=d, depth=depth, nb=nb, seq=seq, db=db, dseq=dseq, tp=nb * seq, t=nb * seq + db * dseq,
                inner=inner, heads=heads, hpg=heads // SSM_GROUPS, gw=inner // SSM_GROUPS, state=state, xbc=xbc,
                qkh=qkh, vh=vh, qkv=qkv, val=val, n_exp=w_gate.shape[1], dff=w_gate.shape[3], in_cols=in_cols,
                off_xbc=off_xbc, off_dt=off_dt, off_qkv=off_qkv, off_zg=off_zg, off_a=off_a, off_b=off_b,
                off_ga=off_ga, off_gb=off_gb, ncond=ncond)


def _tile(pref, *sizes):
    g = 0
    for s in sizes:
        g = math.gcd(g, s)
    t = math.gcd(pref, g)
    return t


def _params(sem, vmem=VMEM_LIMIT):
    return pltpu.CompilerParams(dimension_semantics=sem, vmem_limit_bytes=vmem)


def _sigmoid(x):
    return 1.0 / (1.0 + jnp.exp(-x))


def _silu(x):
    return x * _sigmoid(x)


def _softplus(x):
    return jnp.maximum(x, 0.0) + jnp.log(1.0 + jnp.exp(-jnp.abs(x)))


def _cond_of_row(dm, row0):
    return jnp.where(row0 < dm.tp, 0, 1 + (row0 - dm.tp) // dm.dseq)


def _ada_kernel(c_ref, w_ref, b_ref, o_ref):
    s = _silu(c_ref[...]).astype(BF16)
    o_ref[...] = jnp.dot(s, w_ref[...].astype(BF16), preferred_element_type=F32) + b_ref[...]


def _ada_mod(dm, cond, w_ada, b_ada):
    n = w_ada.shape[2]
    tn = _tile(512, n)
    return pl.pallas_call(
        _ada_kernel,
        grid=(dm.depth, n // tn),
        in_specs=[pl.BlockSpec((dm.ncond, dm.d), lambda l, j: (0, 0)),
                  pl.BlockSpec((None, dm.d, tn), lambda l, j: (l, 0, j)),
                  pl.BlockSpec((None, 1, tn), lambda l, j: (l, 0, j))],
        out_specs=pl.BlockSpec((None, dm.ncond, tn), lambda l, j: (l, 0, j)),
        out_shape=jax.ShapeDtypeStruct((dm.depth, dm.ncond, n), F32),
        compiler_params=_params(("arbitrary", "arbitrary")),
        name="ada_mod",
    )(cond, w_ada, b_ada.reshape(dm.depth, 1, n))


def _mod_spec(dm, tm, l, which, width=None, col_of=None):
    width = dm.d if width is None else width
    per = dm.d // width

    def idx(*g):
        i, j = (g[0], 0) if col_of is None else col_of(*g)
        return (l, _cond_of_row(dm, i * tm), 0, which * per + j)

    return pl.BlockSpec((None, None, 1, width), idx)


def _embed_kernel(xp_ref, xs_ref, pos_ref, sh_ref, sc_ref, x_ref, h_ref, *, npb):
    is_ctx = pl.program_id(0) < npb
    x = jnp.where(is_ctx, xp_ref[...], xs_ref[...] + pos_ref[...])
    x_ref[...] = x
    h_ref[...] = (x * (1.0 + sc_ref[...]) + sh_ref[...]).astype(BF16)


def _embed(dm, xp, xs, pos, mod4):
    tm = _tile(256, dm.seq, dm.dseq)
    npb = dm.tp // tm
    nsb = (dm.t - dm.tp) // tm
    ppb = dm.dseq // tm
    row = lambda i: (i, 0)
    return pl.pallas_call(
        functools.partial(_embed_kernel, npb=npb),
        grid=(dm.t // tm,),
        in_specs=[pl.BlockSpec((tm, dm.d), lambda i: (jnp.minimum(i, npb - 1), 0)),
                  pl.BlockSpec((tm, dm.d), lambda i: (jnp.clip(i - npb, 0, nsb - 1), 0)),
                  pl.BlockSpec((tm, dm.d), lambda i: (jnp.maximum(i - npb, 0) % ppb, 0)),
                  _mod_spec(dm, tm, 0, 0), _mod_spec(dm, tm, 0, 1)],
        out_specs=[pl.BlockSpec((tm, dm.d), row), pl.BlockSpec((tm, dm.d), row)],
        out_shape=[jax.ShapeDtypeStruct((dm.t, dm.d), F32), jax.ShapeDtypeStruct((dm.t, dm.d), BF16)],
        compiler_params=_params(("arbitrary",)),
        name="embed_mod",
    )(xp, xs, pos, mod4, mod4)


def _mm_res_kernel(x_ref, w_ref, r_ref, g_ref, o_ref, wb_ref, *, alpha):
    @pl.when(pl.program_id(1) == 0)
    def _():
        wb_ref[...] = w_ref[...].astype(BF16)

    acc = jnp.dot(x_ref[...], wb_ref[...], preferred_element_type=F32)
    o_ref[...] = alpha * r_ref[...] + g_ref[...] * acc


def _mm_stream_kernel(x_ref, w_ref, o_ref):
    o_ref[...] = jnp.dot(x_ref[...], w_ref[...].astype(BF16), preferred_element_type=F32).astype(o_ref.dtype)


def _in_proj(dm, h, w_in, l):
    k = dm.d
    n = dm.in_cols
    tm = _tile(2048, dm.t)
    tn = _tile(256, n)
    return pl.pallas_call(
        _mm_stream_kernel,
        grid=(dm.t // tm, n // tn),
        in_specs=[pl.BlockSpec((tm, k), lambda i, j: (i, 0)),
                  pl.BlockSpec((None, k, tn), lambda i, j: (l, 0, j))],
        out_specs=pl.BlockSpec((tm, tn), lambda i, j: (i, j)),
        out_shape=jax.ShapeDtypeStruct((dm.t, n), F32),
        compiler_params=_params(("arbitrary", "arbitrary")),
        name="in_proj",
    )(h, w_in)


def _out_proj(dm, merged, w_out, x, mod4, l, alpha):
    k = dm.d
    tm = _tile(1024, dm.tp, dm.dseq)
    tn = _tile(512, dm.d)
    return pl.pallas_call(
        functools.partial(_mm_res_kernel, alpha=alpha),
        grid=(dm.d // tn, dm.t // tm),
        in_specs=[pl.BlockSpec((tm, k), lambda j, i: (i, 0)),
                  pl.BlockSpec((None, k, tn), lambda j, i: (l, 0, j)),
                  pl.BlockSpec((tm, tn), lambda j, i: (i, j)),
                  _mod_spec(dm, tm, l, 2, width=tn, col_of=lambda j, i: (i, j))],
        out_specs=pl.BlockSpec((tm, tn), lambda j, i: (i, j)),
        out_shape=jax.ShapeDtypeStruct((dm.t, dm.d), F32),
        scratch_shapes=[pltpu.VMEM((k, tn), BF16)],
        compiler_params=_params(("arbitrary", "arbitrary")),
        name="out_proj",
    )(merged, w_out, x, mod4)


def _branch_kernel(ya_ref, yb_ref, wa_ref, wb_ref, ga_ref, gb_ref, o_ref, acc_ref, res_ref, wpa_ref, wpb_ref,
                   *, nk, tk):
    i = pl.program_id(1)
    kb = pl.program_id(2)

    @pl.when(kb % nk == 0)
    def _():
        acc_ref[...] = jnp.zeros_like(acc_ref)

    def accumulate(y_ref, w_ref, wp_ref, kk):
        rows = pl.ds(pl.multiple_of(kk * tk, tk), tk)

        @pl.when(i == 0)
        def _():
            wp_ref[rows, :] = w_ref[...].astype(BF16)

        acc_ref[...] += jnp.dot(y_ref[...], wp_ref[rows, :], preferred_element_type=F32)

    @pl.when(kb < nk)
    def _():
        accumulate(ya_ref, wa_ref, wpa_ref, kb)

    @pl.when(kb >= nk)
    def _():
        accumulate(yb_ref, wb_ref, wpb_ref, kb - nk)

    @pl.when(kb == nk - 1)
    def _():
        res_ref[...] = _sigmoid(ga_ref[...]) * acc_ref[...]

    @pl.when(kb == 2 * nk - 1)
    def _():
        o_ref[...] = (res_ref[...] + _sigmoid(gb_ref[...]) * acc_ref[...]).astype(o_ref.dtype)


def _branch_merge(dm, y_ssm, y_gdn, w_a, w_b, proj, l):
    assert dm.inner == dm.val
    kdim = dm.inner
    tm = _tile(1024, dm.t)
    tn = _tile(512, dm.d, dm.off_ga, dm.off_gb)
    tk = _tile(1024, kdim)
    nk = kdim // tk
    ca, cb = dm.off_ga // tn, dm.off_gb // tn

    def wa_idx(j, i, kb):
        return (l, jnp.where(i == 0, jnp.minimum(kb, nk - 1), nk - 1), j)

    def wb_idx(j, i, kb):
        return (l, jnp.where(i == 0, jnp.maximum(kb - nk, 0), nk - 1), j)

    return pl.pallas_call(
        functools.partial(_branch_kernel, nk=nk, tk=tk),
        grid=(dm.d // tn, dm.t // tm, 2 * nk),
        in_specs=[pl.BlockSpec((tm, tk), lambda j, i, kb: (i, jnp.minimum(kb, nk - 1))),
                  pl.BlockSpec((tm, tk), lambda j, i, kb: (i, jnp.maximum(kb - nk, 0))),
                  pl.BlockSpec((None, tk, tn), wa_idx),
                  pl.BlockSpec((None, tk, tn), wb_idx),
                  pl.BlockSpec((tm, tn), lambda j, i, kb: (i, ca + j)),
                  pl.BlockSpec((tm, tn), lambda j, i, kb: (i, cb + j))],
        out_specs=pl.BlockSpec((tm, tn), lambda j, i, kb: (i, j)),
        out_shape=jax.ShapeDtypeStruct((dm.t, dm.d), BF16),
        scratch_shapes=[pltpu.VMEM((tm, tn), F32), pltpu.VMEM((tm, tn), F32),
                        pltpu.VMEM((kdim, tn), BF16), pltpu.VMEM((kdim, tn), BF16)],
        compiler_params=_params(("arbitrary", "arbitrary", "arbitrary")),
        name="branch_merge",
    )(y_ssm, y_gdn, w_a, w_b, proj, proj)


def _gates_kernel(dtr_ref, ar_ref, br_ref, dtb_ref, gb_ref, al_ref, dt_ref, g_ref, beta_ref):
    dt_ref[...] = _softplus(dtr_ref[...] + dtb_ref[...])
    g_ref[...] = -jnp.exp(al_ref[...]) * _softplus(ar_ref[...] + gb_ref[...])
    beta_ref[...] = _sigmoid(br_ref[...])


def _gates(dm, dt_raw, a_raw, b_raw, dt_bias, g_bias, a_log):
    tm = _tile(512, dm.t)
    w1, w2 = 2 * dm.heads, 2 * dm.vh
    row = lambda i: (i, 0)
    fix = lambda i: (0, 0)
    return pl.pallas_call(
        _gates_kernel,
        grid=(dm.t // tm,),
        in_specs=[pl.BlockSpec((tm, w1), row), pl.BlockSpec((tm, w2), row), pl.BlockSpec((tm, w2), row),
                  pl.BlockSpec((1, w1), fix), pl.BlockSpec((1, w2), fix), pl.BlockSpec((1, w2), fix)],
        out_specs=[pl.BlockSpec((tm, w1), row), pl.BlockSpec((tm, w2), row), pl.BlockSpec((tm, w2), row)],
        out_shape=[jax.ShapeDtypeStruct((dm.t, w1), F32), jax.ShapeDtypeStruct((dm.t, w2), F32),
                   jax.ShapeDtypeStruct((dm.t, w2), F32)],
        compiler_params=_params(("arbitrary",)),
        name="gate_prep",
    )(dt_raw, a_raw, b_raw, dt_bias.reshape(1, w1), g_bias.reshape(1, w2), a_log.reshape(1, w2))


def _conv_kernel(*refs, rows, blocks_ctx, blocks_per_ctx_seq, blocks_per_lat_seq, has_bias):
    if has_bias:
        x_ref, prev_ref, next_ref, w_ref, b_ref, o_ref = refs
    else:
        x_ref, prev_ref, next_ref, w_ref, o_ref = refs
        b_ref = None
    i = pl.program_id(0)
    in_ctx = i < blocks_ctx
    pos = jnp.where(in_ctx, i % blocks_per_ctx_seq, (i - blocks_ctx) % blocks_per_lat_seq)
    per = jnp.where(in_ctx, blocks_per_ctx_seq, blocks_per_lat_seq)
    prev = jnp.where(pos == 0, 0.0, prev_ref[...])
    nxt = jnp.where(pos == per - 1, 0.0, next_ref[...])
    ext = jnp.concatenate([prev, x_ref[...], nxt], axis=0)
    n = rows + 2 * SUBLANES
    pad = CONV_K // 2
    w = w_ref[...]
    acc = None
    for tap in range(CONV_K):
        start = SUBLANES - pad + tap
        shifted = ext if start == 0 else pltpu.roll(ext, n - start, 0)
        term = shifted[:rows] * w[tap:tap + 1, :]
        acc = term if acc is None else acc + term
    if has_bias:
        acc = acc + b_ref[...]
    o_ref[...] = _silu(acc)


def _conv_act(dm, proj, col_off, width, w, b, l):
    rows = _tile(256, dm.seq, dm.dseq)
    tc = _tile(2048, col_off, width)
    c0 = col_off // tc
    hb = rows // SUBLANES
    nhalo = dm.t // SUBLANES
    has_bias = b is not None
    in_specs = [pl.BlockSpec((rows, tc), lambda i, j: (i, c0 + j)),
                pl.BlockSpec((SUBLANES, tc), lambda i, j: (jnp.maximum(i * hb - 1, 0), c0 + j)),
                pl.BlockSpec((SUBLANES, tc), lambda i, j: (jnp.minimum((i + 1) * hb, nhalo - 1), c0 + j)),
                pl.BlockSpec((None, CONV_K, tc), lambda i, j: (l, 0, j))]
    args = [proj, proj, proj, w]
    if has_bias:
        in_specs.append(pl.BlockSpec((None, 1, tc), lambda i, j: (l, 0, j)))
        args.append(b.reshape(dm.depth, 1, width))
    return pl.pallas_call(
        functools.partial(_conv_kernel, rows=rows, blocks_ctx=dm.tp // rows, blocks_per_ctx_seq=dm.seq // rows,
                          blocks_per_lat_seq=dm.dseq // rows, has_bias=has_bias),
        grid=(dm.t // rows, width // tc),
        in_specs=in_specs,
        out_specs=pl.BlockSpec((rows, tc), lambda i, j: (i, j)),
        out_shape=jax.ShapeDtypeStruct((dm.t, width), F32),
        compiler_params=_params(("arbitrary", "arbitrary")),
        name="conv_act",
    )(*args)


def _ssd_kernel(*refs, nc, hpg, hd, has_state, want_state):
    it = iter(refs)
    xs_ref, b_ref, c_ref, dtr_ref, alr_ref, z_ref, dsk_ref, nw_ref = (next(it) for _ in range(8))
    h0_ref = next(it) if has_state else None
    y_ref = next(it)
    hout_ref = next(it) if want_state else None
    ht_ref, yf_ref, xw_ref, yc_ref = (next(it) for _ in range(4))

    d = pl.program_id(2)
    c = pl.program_id(3)
    q = xs_ref.shape[0]
    fwd = d == 0
    ii = lax.broadcasted_iota(jnp.int32, (q, q), 0)
    jj = lax.broadcasted_iota(jnp.int32, (q, q), 1)
    mask = ((ii - jj) * (1 - 2 * d)) >= 0
    tri = mask.astype(F32)

    dt_r = dtr_ref[...]
    a_r = dt_r * (-jnp.exp(alr_ref[...]))
    acum_r = lax.dot_general(a_r, tri, (((1,), (1,)), ((), ())), precision=HI, preferred_element_type=F32)
    alast_r = jnp.where(fwd, acum_r[:, q - 1:q], acum_r[:, 0:1])
    assert q == LANES and 2 * hpg <= LANES
    cols = jnp.concatenate([acum_r, dt_r, jnp.zeros((LANES - 2 * hpg, q), F32)], axis=0).T
    acum_c = cols[:, :hpg]
    dt_c = cols[:, hpg:2 * hpg]
    alast_c = jnp.where(fwd, acum_c[q - 1:q, :], acum_c[0:1, :])

    bm = b_ref[...].astype(BF16)
    cm = c_ref[...].astype(BF16)
    cb = lax.dot_general(cm, bm, (((1,), (1,)), ((), ())), preferred_element_type=F32)

    @pl.when(c == 0)
    def _():
        if has_state:
            ht_ref[...] = h0_ref[...].T
        else:
            ht_ref[...] = jnp.zeros_like(ht_ref)

    ht = ht_ref[...]
    yo = jnp.dot(cm, ht.astype(BF16), preferred_element_type=F32)
    xs = xs_ref[...]
    log_end = alast_c - acum_c + jnp.log(dt_c)
    for r0 in range(0, hpg, SSD_HEADS_PER_STAGE):
        heads = range(r0, min(r0 + SSD_HEADS_PER_STAGE, hpg))
        sls = [slice(r * hd, (r + 1) * hd) for r in heads]
        acs = [jnp.broadcast_to(acum_c[:, r:r + 1], (q, q)) for r in heads]
        les = [jnp.broadcast_to(log_end[:, r:r + 1], (q, hd)) for r in heads]
        decs = [jnp.where(mask, jnp.exp(ac - acum_r[r:r + 1, :]), 0.0) for r, ac in zip(heads, acs)]
        wgts = [(cb * dec * dt_r[r:r + 1, :]).astype(BF16) for r, dec in zip(heads, decs)]
        xrs = [xs[:, sl] for sl in sls]
        yds = [jnp.dot(wgt, x_r.astype(BF16), preferred_element_type=F32) for wgt, x_r in zip(wgts, xrs)]
        for sl, ac, le, x_r, yd in zip(sls, acs, les, xrs, yds):
            yc_ref[:, sl] = yd + yo[:, sl] * jnp.exp(ac[:, :hd])
            xw_ref[:, sl] = (x_r * jnp.exp(le)).astype(BF16)
    st = lax.dot_general(bm, xw_ref[...], (((0,), (0,)), ((), ())), preferred_element_type=F32)
    for r in range(hpg):
        sl = slice(r * hd, (r + 1) * hd)
        ht_ref[:, sl] = ht[:, sl] * jnp.exp(alast_r[r:r + 1, :]) + st[:, sl]

    cidx = c + d * (nc - 1 - 2 * c)
    rows = pl.ds(pl.multiple_of(cidx * q, q), q)

    @pl.when(fwd)
    def _():
        yf_ref[rows, :] = yc_ref[...]

    @pl.when(d == 1)
    def _():
        tot = yf_ref[rows, :] + yc_ref[...] + dsk_ref[...] * xs
        gated = tot * _silu(z_ref[...])
        ms = jnp.mean(gated * gated, axis=-1, keepdims=True)
        y_ref[...] = (gated * lax.rsqrt(ms + RMS_EPS) * nw_ref[...]).astype(y_ref.dtype)

    if want_state:
        @pl.when(c == nc - 1)
        def _():
            hout_ref[...] = ht_ref[...].T


def _ssd_call(dm, act, proj, dtr, alr, dskip, nw, y_prev, h0, l, row_off, nseq, length):
    q = min(SCAN_CHUNK, length)
    nc = length // q
    base = row_off // q
    g_n, gw, n = SSM_GROUPS, dm.gw, dm.state
    has_state = h0 is not None
    want_state = not has_state
    bcol = dm.inner // n

    def rb(b, d, c):
        return base + b * nc + c + d * (nc - 1 - 2 * c)

    def rb_out(b, d, c):
        return base + b * nc + jnp.where(d == 0, nc - 1, nc - 1 - c)

    in_specs = [pl.BlockSpec((q, gw), lambda b, g, d, c: (rb(b, d, c), g)),
                pl.BlockSpec((q, n), lambda b, g, d, c: (rb(b, d, c), bcol + g)),
                pl.BlockSpec((q, n), lambda b, g, d, c: (rb(b, d, c), bcol + g_n + g)),
                pl.BlockSpec((None, None, dm.hpg, q), lambda b, g, d, c: (d, g, 0, rb(b, d, c))),
                pl.BlockSpec((None, None, dm.hpg, 1), lambda b, g, d, c: (d, g, 0, 0)),
                pl.BlockSpec((q, gw), lambda b, g, d, c: (rb_out(b, d, c), g)),
                pl.BlockSpec((1, gw), lambda b, g, d, c: (0, g)),
                pl.BlockSpec((1, gw), lambda b, g, d, c: (0, g))]
    args = [act, act, act, dtr, alr, proj, dskip, nw]
    if has_state:
        in_specs.append(pl.BlockSpec((None, None, None, None, gw, n), lambda b, g, d, c: (b, l, d, g, 0, 0)))
        args.append(h0)
    out_specs = [pl.BlockSpec((q, gw), lambda b, g, d, c: (rb_out(b, d, c), g))]
    out_shape = [jax.ShapeDtypeStruct((dm.t, dm.inner), BF16)]
    if want_state:
        out_specs.append(pl.BlockSpec((None, None, None, gw, n), lambda b, g, d, c: (b, d, g, 0, 0)))
        out_shape.append(jax.ShapeDtypeStruct((nseq, 2, g_n, gw, n), F32))
    aliases = {}
    if y_prev is not None:
        in_specs.append(pl.BlockSpec(memory_space=pl.ANY))
        args.append(y_prev)
        aliases = {len(args) - 1: 0}

    def body(*refs):
        refs = list(refs)
        if y_prev is not None:
            n_in = len(args)
            del refs[n_in - 1]
        _ssd_kernel(*refs, nc=nc, hpg=dm.hpg, hd=SSM_HEADDIM, has_state=has_state, want_state=want_state)

    return pl.pallas_call(
        body,
        grid=(nseq, g_n, 2, nc),
        in_specs=in_specs,
        out_specs=out_specs,
        out_shape=out_shape,
        scratch_shapes=[pltpu.VMEM((n, gw), F32), pltpu.VMEM((length, gw), F32),
                        pltpu.VMEM((q, gw), BF16), pltpu.VMEM((q, gw), F32)],
        input_output_aliases=aliases,
        compiler_params=_params(("arbitrary",) * 4),
        name="ssd_scan",
    )(*args)


def _conv_silu_seq(x, w):
    n = x.shape[0]
    pad = CONV_K // 2
    t = lax.broadcasted_iota(jnp.int32, x.shape, 0)
    acc = x * w[pad:pad + 1, :]
    for tap in range(CONV_K):
        off = tap - pad
        if off == 0:
            continue
        shifted = pltpu.roll(x, (-off) % n, 0)
        inside = (t >= -off) if off < 0 else (t < n - off)
        acc = acc + jnp.where(inside, shifted, 0.0) * w[tap:tap + 1, :]
    return _silu(acc)


def _gdn_kernel(*refs, nc, cl, has_state, want_state):
    it = iter(refs)
    q_ref, k_ref, v0_ref, v1_ref, wq_ref, wk_ref, wv0_ref, wv1_ref = (next(it) for _ in range(8))
    z0_ref, z1_ref, grow_ref, nw_ref = (next(it) for _ in range(4))
    s0_ref = next(it) if has_state else None
    y_ref = next(it)
    sout_ref = next(it) if want_state else None
    qn_ref, kn_ref, va_ref, t_ref, qk_ref, gc_ref, o_ref, s_ref = (next(it) for _ in range(8))
    z_refs = (z0_ref, z1_ref)
    dk = q_ref.shape[1]
    dv = v0_ref.shape[1]

    qv = _conv_silu_seq(q_ref[...], wq_ref[...])
    qn_ref[...] = (qv * lax.rsqrt(jnp.sum(qv * qv, -1, keepdims=True) + RMS_EPS) * (dk ** -0.5)).astype(BF16)
    kv = _conv_silu_seq(k_ref[...], wk_ref[...])
    kn_ref[...] = (kv * lax.rsqrt(jnp.sum(kv * kv, -1, keepdims=True) + RMS_EPS)).astype(BF16)
    va_ref[0] = _conv_silu_seq(v0_ref[...], wv0_ref[...])
    va_ref[1] = _conv_silu_seq(v1_ref[...], wv1_ref[...])

    ii = lax.broadcasted_iota(jnp.int32, (cl, cl), 0)
    jj = lax.broadcasted_iota(jnp.int32, (cl, cl), 1)
    eye = (ii == jj).astype(F32)
    tri_f = (ii >= jj).astype(F32)
    incl = (ii >= jj, ii <= jj)
    strict = (ii > jj, ii < jj)
    assert cl == LANES
    sub = lax.broadcasted_iota(jnp.int32, (SUBLANES, cl), 0)
    n_lvl = int(math.log2(cl))
    pair_mask = [((ii >> (k + 1)) == (jj >> (k + 1))) & ((ii >> k) != (jj >> k)) for k in range(n_lvl)]
    nt = (((1,), (1,)), ((), ()))

    ua = 4 if nc % 4 == 0 else (2 if nc % 2 == 0 else 1)

    def phase_a(t, carry):
        nms, tms, slots = [], [], []
        for u in range(ua):
            ci = t * ua + u
            rows = pl.ds(pl.multiple_of(ci * cl, cl), cl)
            kc = kn_ref[rows, :]
            qc = qn_ref[rows, :]
            kk = lax.dot_general(kc, kc, nt, preferred_element_type=F32)
            qk = lax.dot_general(qc, kc, nt, preferred_element_type=F32)
            grow = grow_ref[ci]
            pre_r = lax.dot_general(grow, tri_f, nt, precision=HI, preferred_element_type=F32)
            suf_r = jnp.sum(grow, axis=1, keepdims=True) - pre_r + grow
            grb = jnp.where(sub >= 4, grow, jnp.where(sub >= 2, suf_r, pre_r))
            gcb = jnp.concatenate([grb, jnp.zeros((LANES - SUBLANES, cl), F32)], axis=0).T
            gc_ref[rows, :] = gcb
            for kx in range(4):
                dd = kx // 2
                gi = gcb[:, kx:kx + 1]
                gj = grb[kx:kx + 1, :]
                bi = gcb[:, 4 + kx:5 + kx]
                dec = jnp.where(incl[dd], jnp.exp(gi - gj), 0.0)
                nm = jnp.where(strict[dd], bi * kk * dec, 0.0)
                qk_ref[ci, kx] = (qk * dec).astype(BF16)
                nms.append(nm)
                tms.append(eye - jnp.where(pair_mask[0], nm, 0.0))
                slots.append((ci, kx))
        for lvl in range(1, n_lvl):
            tbs = [tm.astype(BF16) for tm in tms]
            cts = [jnp.dot(jnp.where(pair_mask[lvl], nm, 0.0).astype(BF16), tb, preferred_element_type=F32)
                   for nm, tb in zip(nms, tbs)]
            tms = [tm - jnp.dot(tb, ct.astype(BF16), preferred_element_type=F32)
                   for tm, tb, ct in zip(tms, tbs, cts)]
        for (ci, kx), tm in zip(slots, tms):
            t_ref[ci, kx] = tm.astype(BF16)
        return carry

    lax.fori_loop(0, nc // ua, phase_a, 0)

    for kx in range(4):
        if has_state:
            s_ref[kx] = s0_ref[kx // 2, kx % 2]
        else:
            s_ref[kx] = jnp.zeros((dk, dv), F32)

    def phase_b(t, carry):
        cidx = (t, t, nc - 1 - t, nc - 1 - t)
        rows = [pl.ds(pl.multiple_of(ci * cl, cl), cl) for ci in cidx]
        kcs = [kn_ref[r, :] for r in rows]
        ss = [s_ref[kx] for kx in range(4)]
        gis = [gc_ref[rows[kx], :][:, kx:kx + 1] for kx in range(4)]
        bis = [gc_ref[rows[kx], :][:, 4 + kx:5 + kx] for kx in range(4)]
        egis = [jnp.exp(gi) for gi in gis]
        kqs = [jnp.dot(jnp.concatenate([kcs[kx], qn_ref[rows[kx], :]], axis=0), ss[kx].astype(BF16),
                       preferred_element_type=F32) for kx in range(4)]
        rhs = [(bis[kx] * (va_ref[kx % 2, rows[kx], :] - egis[kx] * kqs[kx][:cl])).astype(BF16) for kx in range(4)]
        vnews = [jnp.dot(t_ref[cidx[kx], kx], rhs[kx], preferred_element_type=F32) for kx in range(4)]
        for kx in range(4):
            o_ref[kx // 2, kx % 2, rows[kx], :] = egis[kx] * kqs[kx][cl:] + jnp.dot(
                qk_ref[cidx[kx], kx], vnews[kx].astype(BF16), preferred_element_type=F32)
        glasts = [gis[kx][cl - 1:cl, :] if kx < 2 else gis[kx][0:1, :] for kx in range(4)]
        vss = [(jnp.exp(glasts[kx] - gis[kx]) * vnews[kx]).astype(BF16) for kx in range(4)]
        for kx in range(4):
            s_ref[kx] = ss[kx] * jnp.exp(glasts[kx]) + lax.dot_general(
                kcs[kx], vss[kx], (((0,), (0,)), ((), ())), preferred_element_type=F32)
        return carry

    lax.fori_loop(0, nc, phase_b, 0)

    for e in range(2):
        o = o_ref[0, e] + o_ref[1, e]
        on = o * lax.rsqrt(jnp.mean(o * o, -1, keepdims=True) + RMS_EPS)
        y_ref[:, e * dv:(e + 1) * dv] = (on * nw_ref[...] * _silu(z_refs[e][...])).astype(y_ref.dtype)
    if want_state:
        for kx in range(4):
            sout_ref[kx // 2, kx % 2] = s_ref[kx]


def _gdn_call(dm, proj, conv_w, grow, nw, y_prev, s0, l, row_off, nseq, length):
    cl = min(SCAN_CHUNK, length)
    nc = length // cl
    assert row_off % length == 0
    base = row_off // length
    dk, dv = GDN_DK, GDN_DV
    has_state = s0 is not None
    want_state = not has_state
    assert dm.off_qkv % dk == 0 and dm.off_zg % dv == 0
    qcol = dm.off_qkv // dk
    kcol = dm.qkh
    vcol = 2 * dm.qkh
    zcol = dm.off_zg // dv

    def col_specs(first, shape, row_of):
        return [pl.BlockSpec(shape, lambda b, j: row_of(b) + (first + j,)),
                pl.BlockSpec(shape, lambda b, j: row_of(b) + (first + kcol + j,)),
                pl.BlockSpec(shape, lambda b, j: row_of(b) + (first + vcol + 2 * j,)),
                pl.BlockSpec(shape, lambda b, j: row_of(b) + (first + vcol + 2 * j + 1,))]

    in_specs = (col_specs(qcol, (length, dk), lambda b: (base + b,))
                + col_specs(0, (None, CONV_K, dk), lambda b: (l, 0))
                + [pl.BlockSpec((length, dv), lambda b, j: (base + b, zcol + 2 * j)),
                   pl.BlockSpec((length, dv), lambda b, j: (base + b, zcol + 2 * j + 1)),
                   pl.BlockSpec((None, nc, SUBLANES, cl), lambda b, j: (j, b, 0, 0)),
                   pl.BlockSpec((1, dv), lambda b, j: (0, 0))])
    args = [proj] * 4 + [conv_w] * 4 + [proj, proj, grow, nw]
    if has_state:
        in_specs.append(pl.BlockSpec((None, None, 2, 2, dk, dv), lambda b, j: (b, l, 0, j, 0, 0)))
        args.append(s0)
    out_specs = [pl.BlockSpec((length, 2 * dv), lambda b, j: (base + b, j))]
    out_shape = [jax.ShapeDtypeStruct((dm.t, dm.val), BF16)]
    if want_state:
        out_specs.append(pl.BlockSpec((None, 2, 2, dk, dv), lambda b, j: (b, 0, j, 0, 0)))
        out_shape.append(jax.ShapeDtypeStruct((nseq, 2, dm.vh, dk, dv), F32))
    aliases = {}
    if y_prev is not None:
        in_specs.append(pl.BlockSpec(memory_space=pl.ANY))
        args.append(y_prev)
        aliases = {len(args) - 1: 0}

    def body(*refs):
        refs = list(refs)
        if y_prev is not None:
            del refs[len(args) - 1]
        _gdn_kernel(*refs, nc=nc, cl=cl, has_state=has_state, want_state=want_state)

    return pl.pallas_call(
        body,
        grid=(nseq, dm.qkh),
        in_specs=in_specs,
        out_specs=out_specs,
        out_shape=out_shape,
        scratch_shapes=[pltpu.VMEM((length, dk), BF16), pltpu.VMEM((length, dk), BF16),
                        pltpu.VMEM((2, length, dv), F32),
                        pltpu.VMEM((nc, 4, cl, cl), BF16), pltpu.VMEM((nc, 4, cl, cl), BF16),
                        pltpu.VMEM((length, LANES), F32),
                        pltpu.VMEM((2, 2, length, dv), F32), pltpu.VMEM((4, dk, dv), F32)],
        input_output_aliases=aliases,
        compiler_params=_params(("arbitrary", "arbitrary")),
        name="gdn_scan",
    )(*args)


def _layer_norm(x, g, b):
    mu = jnp.mean(x, -1, keepdims=True)
    xc = x - mu
    var = jnp.mean(xc * xc, -1, keepdims=True)
    return xc * lax.rsqrt(var + LN_EPS) * g + b


def _ln_router_kernel(pre_ref, g_ref, b_ref, sh_ref, sc_ref, rw_ref, x_ref, h_ref, lg_ref):
    x = _layer_norm(pre_ref[...], g_ref[...], b_ref[...])
    x_ref[...] = x
    h = x * (1.0 + sc_ref[...]) + sh_ref[...]
    h_ref[...] = h
    lg_ref[...] = jnp.dot(h, rw_ref[...], precision=HI, preferred_element_type=F32)


def _ln_router(dm, pre, g, b, mod4, rw_pad, l):
    tm = _tile(256, dm.seq, dm.dseq)
    row = lambda i: (i, 0)
    fix = lambda i: (0, 0)
    ne = rw_pad.shape[1]
    return pl.pallas_call(
        _ln_router_kernel,
        grid=(dm.t // tm,),
        in_specs=[pl.BlockSpec((tm, dm.d), row),
                  pl.BlockSpec((None, 1, dm.d), lambda i: (l, 0, 0)),
                  pl.BlockSpec((None, 1, dm.d), lambda i: (l, 0, 0)),
                  _mod_spec(dm, tm, l, 3), _mod_spec(dm, tm, l, 4),
                  pl.BlockSpec((dm.d, ne), fix)],
        out_specs=[pl.BlockSpec((tm, dm.d), row), pl.BlockSpec((tm, dm.d), row), pl.BlockSpec((tm, ne), row)],
        out_shape=[jax.ShapeDtypeStruct((dm.t, dm.d), F32), jax.ShapeDtypeStruct((dm.t, dm.d), F32),
                   jax.ShapeDtypeStruct((dm.t, ne), F32)],
        compiler_params=_params(("arbitrary",)),
        name="ln1_router",
    )(pre, g.reshape(dm.depth, 1, dm.d), b.reshape(dm.depth, 1, dm.d), mod4, mod4, rw_pad)


def _ln_res_kernel(*refs, alpha, with_mod, tm):
    if with_mod:
        slot_ref, x_ref, yb_ref, sel_ref, gt_ref, g_ref, b_ref, sh_ref, sc_ref, o_ref, h_ref, rows_ref, sem = refs
    else:
        slot_ref, x_ref, yb_ref, sel_ref, gt_ref, g_ref, b_ref, o_ref, rows_ref, sem = refs
    base = pl.program_id(0) * tm

    def copy(s):
        tok, k = s // TOP_K, s % TOP_K
        return _row_copy(yb_ref, slot_ref[(base + tok) * TOP_K + k], rows_ref.at[k], tok, sem)

    def start(s, carry):
        copy(s).start()
        return carry

    def wait(s, carry):
        copy(s).wait()
        return carry

    lax.fori_loop(0, tm * TOP_K, start, 0)
    lax.fori_loop(0, tm * TOP_K, wait, 0)
    sel = sel_ref[...]
    y = rows_ref[0] * sel[:, 2 * TOP_K:2 * TOP_K + 1]
    for k in range(1, TOP_K):
        y = y + rows_ref[k] * sel[:, 2 * TOP_K + k:2 * TOP_K + k + 1]
    x = _layer_norm(alpha * x_ref[...] + gt_ref[...] * y, g_ref[...], b_ref[...])
    o_ref[...] = x
    if with_mod:
        h_ref[...] = (x * (1.0 + sc_ref[...]) + sh_ref[...]).astype(BF16)


def _ln_res(dm, x, yb, slot_dest, sel, g, b, mod4, l, alpha):
    tm = _tile(256, dm.seq, dm.dseq)
    row = lambda i, s: (i, 0)
    with_mod = l + 1 < dm.depth
    ne = sel.shape[1]

    def mod_spec(layer, which):
        inner = _mod_spec(dm, tm, layer, which)
        return pl.BlockSpec(inner.block_shape, lambda i, s: inner.index_map(i))

    in_specs = [pl.BlockSpec((tm, dm.d), row), pl.BlockSpec(memory_space=pl.ANY), pl.BlockSpec((tm, ne), row),
                mod_spec(l, 5),
                pl.BlockSpec((None, 1, dm.d), lambda i, s: (l, 0, 0)),
                pl.BlockSpec((None, 1, dm.d), lambda i, s: (l, 0, 0))]
    args = [x, yb, sel, mod4, g.reshape(dm.depth, 1, dm.d), b.reshape(dm.depth, 1, dm.d)]
    out_specs = [pl.BlockSpec((tm, dm.d), row)]
    out_shape = [jax.ShapeDtypeStruct((dm.t, dm.d), F32)]
    if with_mod:
        in_specs += [mod_spec(l + 1, 0), mod_spec(l + 1, 1)]
        args += [mod4, mod4]
        out_specs.append(pl.BlockSpec((tm, dm.d), row))
        out_shape.append(jax.ShapeDtypeStruct((dm.t, dm.d), BF16))
    return pl.pallas_call(
        functools.partial(_ln_res_kernel, alpha=alpha, with_mod=with_mod, tm=tm),
        grid_spec=pltpu.PrefetchScalarGridSpec(
            num_scalar_prefetch=1,
            grid=(dm.t // tm,),
            in_specs=in_specs,
            out_specs=out_specs,
            scratch_shapes=[pltpu.VMEM((TOP_K, tm, dm.d), F32), pltpu.SemaphoreType.DMA(())]),
        out_shape=out_shape,
        compiler_params=_params(("arbitrary",)),
        name="ln2",
    )(slot_dest, *args)


def _ffn_up_kernel(e_ref, nu_ref, x_ref, wg_ref, wu_ref, h_ref, wgb_ref, wub_ref):
    i = pl.program_id(1)
    prev = e_ref[jnp.maximum(i - 1, 0)]

    @pl.when((i < nu_ref[0]) & ((i == 0) | (e_ref[i] != prev)))
    def _():
        wgb_ref[...] = wg_ref[...].astype(BF16)
        wub_ref[...] = wu_ref[...].astype(BF16)

    @pl.when(i < nu_ref[0])
    def _():
        x = x_ref[...].astype(BF16)
        a = jnp.dot(x, wgb_ref[...], preferred_element_type=F32)
        u = jnp.dot(x, wub_ref[...], preferred_element_type=F32)
        h_ref[...] = (_silu(a) * u).astype(h_ref.dtype)

    @pl.when(i >= nu_ref[0])
    def _():
        h_ref[...] = jnp.zeros_like(h_ref)


def _ffn_down_kernel(e_ref, nu_ref, h_ref, wd_ref, y_ref, wdb_ref):
    i = pl.program_id(1)
    prev = e_ref[jnp.maximum(i - 1, 0)]

    @pl.when((i < nu_ref[0]) & ((i == 0) | (e_ref[i] != prev)))
    def _():
        wdb_ref[...] = wd_ref[...].astype(BF16)

    @pl.when(i < nu_ref[0])
    def _():
        y_ref[...] = jnp.dot(h_ref[...], wdb_ref[...], preferred_element_type=F32)

    @pl.when(i >= nu_ref[0])
    def _():
        y_ref[...] = jnp.zeros_like(y_ref)


def _expert_ffn(dm, xb, blk_e, n_used, w_gate, w_up, w_down, l):
    n_buf = xb.shape[0]
    n_blk = n_buf // MOE_ROWS
    tf = _tile(512, dm.dff)
    tn = _tile(2048, dm.d)

    def blk(i, nu):
        return jnp.minimum(i, nu[0] - 1)

    hid = pl.pallas_call(
        _ffn_up_kernel,
        grid_spec=pltpu.PrefetchScalarGridSpec(
            num_scalar_prefetch=2,
            grid=(dm.dff // tf, n_blk),
            in_specs=[pl.BlockSpec((MOE_ROWS, dm.d), lambda f, i, e, nu: (blk(i, nu), 0)),
                      pl.BlockSpec((None, None, dm.d, tf), lambda f, i, e, nu: (l, e[i], 0, f)),
                      pl.BlockSpec((None, None, dm.d, tf), lambda f, i, e, nu: (l, e[i], 0, f))],
            out_specs=pl.BlockSpec((MOE_ROWS, tf), lambda f, i, e, nu: (i, f)),
            scratch_shapes=[pltpu.VMEM((dm.d, tf), BF16), pltpu.VMEM((dm.d, tf), BF16)]),
        out_shape=jax.ShapeDtypeStruct((n_buf, dm.dff), BF16),
        compiler_params=_params(("arbitrary", "arbitrary")),
        name="ffn_up",
    )(blk_e, n_used, xb, w_gate, w_up)

    return pl.pallas_call(
        _ffn_down_kernel,
        grid_spec=pltpu.PrefetchScalarGridSpec(
            num_scalar_prefetch=2,
            grid=(dm.d // tn, n_blk),
            in_specs=[pl.BlockSpec((MOE_ROWS, dm.dff), lambda j, i, e, nu: (blk(i, nu), 0)),
                      pl.BlockSpec((None, None, dm.dff, tn), lambda j, i, e, nu: (l, e[i], 0, j))],
            out_specs=pl.BlockSpec((MOE_ROWS, tn), lambda j, i, e, nu: (i, j)),
            scratch_shapes=[pltpu.VMEM((dm.dff, tn), BF16)]),
        out_shape=jax.ShapeDtypeStruct((n_buf, dm.d), F32),
        compiler_params=_params(("arbitrary", "arbitrary")),
        name="ffn_down",
    )(blk_e, n_used, hid, w_down)


def _route_kernel(lg_ref, rb_ref, o_ref, cnt_ref, carry_ref, *, n_exp, gsz):
    @pl.when(pl.program_id(0) == 0)
    def _():
        carry_ref[...] = jnp.zeros_like(carry_ref)

    tm = lg_ref.shape[0]
    lane_i = lax.broadcasted_iota(jnp.int32, (tm, LANES), 1)
    lane = lane_i.astype(F32)
    valid = lane_i < n_exp
    pos = lane_i % gsz
    neg = -jnp.inf
    big = float(LANES)
    scores = _sigmoid(lg_ref[...])
    biased = jnp.where(valid, scores + rb_ref[...], neg)

    def member(s):
        fwd = pltpu.roll(biased, LANES - s, 1)
        back = pltpu.roll(biased, gsz - s, 1)
        return jnp.where(pos + s < gsz, fwd, back)

    xs = [biased] + [member(s) for s in range(1, gsz)]
    gscore = None
    for a in range(gsz):
        for b in range(a + 1, gsz):
            pair = xs[a] + xs[b]
            gscore = pair if gscore is None else jnp.maximum(gscore, pair)
    gscore = jnp.where(valid, gscore, neg)
    gmax = jnp.max(gscore, axis=1, keepdims=True)
    grp = (lane_i // gsz).astype(F32)
    sel = jnp.min(jnp.where(gscore == gmax, grp, big), axis=1, keepdims=True)
    m = jnp.where((grp == sel) & valid, biased, neg)
    m1 = jnp.max(m, axis=1, keepdims=True)
    i1 = jnp.min(jnp.where(m == m1, lane, big), axis=1, keepdims=True)
    m = jnp.where(lane == i1, neg, m)
    m2 = jnp.max(m, axis=1, keepdims=True)
    i2 = jnp.min(jnp.where(m == m2, lane, big), axis=1, keepdims=True)
    oh1 = lane == i1
    oh2 = lane == i2
    s1 = jnp.sum(jnp.where(oh1, scores, 0.0), axis=1, keepdims=True)
    s2 = jnp.sum(jnp.where(oh2, scores, 0.0), axis=1, keepdims=True)
    both = oh1.astype(F32) + oh2.astype(F32)
    ri = lax.broadcasted_iota(jnp.int32, (tm, tm), 0)
    ci = lax.broadcasted_iota(jnp.int32, (tm, tm), 1)
    earlier = (ri > ci).astype(BF16)
    tot = carry_ref[...] + jnp.dot(earlier, both.astype(BF16), preferred_element_type=F32)
    r1 = jnp.sum(jnp.where(oh1, tot, 0.0), axis=1, keepdims=True)
    r2 = jnp.sum(jnp.where(oh2, tot, 0.0), axis=1, keepdims=True)
    carry_ref[...] += jnp.sum(both, axis=0, keepdims=True)
    cnt_ref[...] = carry_ref[...]
    out = jnp.zeros((tm, LANES), F32)
    for k, val in enumerate((i1, i2, r1, r2, s1 / (s1 + s2), s2 / (s1 + s2))):
        out = jnp.where(lane_i == k, val, out)
    o_ref[...] = out


def _route(dm, logits, router_b):
    n_tok = dm.t
    tm = _tile(256, n_tok)
    ne = logits.shape[1]
    rb = jnp.pad(router_b.astype(F32), (0, ne - dm.n_exp)).reshape(1, ne)
    sel, cnt = pl.pallas_call(
        functools.partial(_route_kernel, n_exp=dm.n_exp, gsz=dm.n_exp // ROUTER_GROUPS),
        grid=(n_tok // tm,),
        in_specs=[pl.BlockSpec((tm, ne), lambda i: (i, 0)), pl.BlockSpec((1, ne), lambda i: (0, 0))],
        out_specs=[pl.BlockSpec((tm, ne), lambda i: (i, 0)), pl.BlockSpec((1, ne), lambda i: (0, 0))],
        out_shape=[jax.ShapeDtypeStruct((n_tok, ne), F32), jax.ShapeDtypeStruct((1, ne), F32)],
        scratch_shapes=[pltpu.VMEM((1, ne), F32)],
        compiler_params=_params(("arbitrary",)),
        name="route",
    )(logits, rb)
    top_e = sel[:, 0:TOP_K].astype(jnp.int32)
    rank = sel[:, TOP_K:2 * TOP_K].astype(jnp.int32)
    counts = cnt[0, :dm.n_exp].astype(jnp.int32)
    padded = (counts + MOE_ROWS - 1) // MOE_ROWS * MOE_ROWS
    pend = jnp.cumsum(padded)
    pstart = pend - padded
    slot_dest = pstart[top_e] + rank
    n_slot = n_tok * TOP_K
    n_buf = -(-n_slot // MOE_ROWS) * MOE_ROWS + dm.n_exp * MOE_ROWS
    n_blk = n_buf // MOE_ROWS
    tok = jnp.repeat(jnp.arange(n_tok, dtype=jnp.int32), TOP_K)
    buf_tok = jnp.full((n_buf,), n_tok, dtype=jnp.int32).at[slot_dest.reshape(n_slot)].set(tok)
    blk_e = jnp.minimum(jnp.searchsorted(pend, jnp.arange(n_blk, dtype=jnp.int32) * MOE_ROWS, side='right'),
                        dm.n_exp - 1).astype(jnp.int32)
    n_used = (pend[-1] // MOE_ROWS).astype(jnp.int32).reshape(1)
    return sel, buf_tok, blk_e, slot_dest, n_used


def _row_copy(src_ref, src_row, dst_ref, dst_row, sem):
    return pltpu.make_async_copy(src_ref.at[pl.ds(src_row, 1)], dst_ref.at[pl.ds(dst_row, 1)], sem)


def _gather_kernel(tok_ref, h_ref, xb_ref, sem, *, rows, n_tok):
    base = pl.program_id(0) * rows

    def copy(r):
        tok = tok_ref[base + r]
        return tok < n_tok, _row_copy(h_ref, jnp.minimum(tok, n_tok - 1), xb_ref, r, sem)

    def start(r, carry):
        is_real, real = copy(r)

        @pl.when(is_real)
        def _():
            real.start()

        @pl.when(jnp.logical_not(is_real))
        def _():
            xb_ref[pl.ds(r, 1), :] = jnp.zeros((1, xb_ref.shape[1]), xb_ref.dtype)

        return carry

    def wait(r, carry):
        is_real, real = copy(r)

        @pl.when(is_real)
        def _():
            real.wait()

        return carry

    lax.fori_loop(0, rows, start, 0)
    lax.fori_loop(0, rows, wait, 0)


def _gather_rows(dm, h, buf_tok):
    n_buf = buf_tok.shape[0]
    rows = _tile(512, n_buf)
    return pl.pallas_call(
        functools.partial(_gather_kernel, rows=rows, n_tok=dm.t),
        grid_spec=pltpu.PrefetchScalarGridSpec(
            num_scalar_prefetch=1,
            grid=(n_buf // rows,),
            in_specs=[pl.BlockSpec(memory_space=pl.ANY)],
            out_specs=pl.BlockSpec((rows, dm.d), lambda i, tok: (i, 0)),
            scratch_shapes=[pltpu.SemaphoreType.DMA(())]),
        out_shape=jax.ShapeDtypeStruct((n_buf, dm.d), h.dtype),
        compiler_params=_params(("arbitrary",)),
        name="moe_gather",
    )(buf_tok, h)


def _moe(dm, h, logits, router_b, w_gate, w_up, w_down, l):
    sel, buf_tok, blk_e, slot_dest, n_used = _route(dm, logits, router_b)
    xb = _gather_rows(dm, h, buf_tok)
    yb = _expert_ffn(dm, xb, blk_e, n_used, w_gate, w_up, w_down, l)
    return yb, slot_dest.reshape(dm.t * TOP_K), sel


def _grid_pos_embed(rows, d):
    nf = d // 4
    omega = 1.0 / (POS_BASE ** (jnp.arange(nf, dtype=F32) / nf))
    ang_r = jnp.arange(rows, dtype=F32)[:, None] * omega
    ang_c = jnp.arange(GRID_W, dtype=F32)[:, None] * omega
    emb_r = jnp.concatenate([jnp.sin(ang_r), jnp.cos(ang_r)], -1)
    emb_c = jnp.concatenate([jnp.sin(ang_c), jnp.cos(ang_c)], -1)
    emb = jnp.concatenate([jnp.broadcast_to(emb_r[:, None], (rows, GRID_W, d // 2)),
                           jnp.broadcast_to(emb_c[None], (rows, GRID_W, d // 2))], -1)
    return emb.reshape(rows * GRID_W, d)


def _ssd_gate_layouts(dm, dt, a_log):
    g_n, hpg = SSM_GROUPS, dm.hpg
    dtr = dt.reshape(dm.t, 2, g_n, hpg).transpose(1, 2, 3, 0)
    alr = a_log.astype(F32).reshape(2, g_n, hpg, 1)
    return dtr, alr


def _gdn_gate_layouts(dm, g, beta, cl_list):
    j_n = dm.qkh
    g4 = g.reshape(dm.t, 2, j_n, 2).transpose(2, 0, 1, 3).reshape(j_n, dm.t, 4)
    b4 = beta.reshape(dm.t, 2, j_n, 2).transpose(2, 0, 1, 3).reshape(j_n, dm.t, 4)
    gb = jnp.concatenate([g4, b4], axis=-1)
    grows = []
    for (row_off, n_rows, cl) in cl_list:
        part = gb[:, row_off:row_off + n_rows]
        grows.append(part.reshape(j_n, n_rows // cl, cl, 8).transpose(0, 1, 3, 2))
    return grows


def kernel(x_prompt, x_sample, state_ssm, state_gdn, c, c_ctx, w_ada, b_ada, w_in, conv_ssm_w, conv_ssm_b, ssm_dt_bias, ssm_a_log, ssm_d, ssm_norm_w, conv_gdn_w, gdn_dt_bias, gdn_a_log, gdn_norm_w, w_branch_ssm, w_branch_gdn, w_out, ln1_g, ln1_b, ln2_g, ln2_b, router_w, router_b, w_gate, w_up, w_down):
    dm = _dims(x_prompt, x_sample, state_ssm, w_in, w_gate)
    alpha = (2.0 * dm.depth) ** 0.25

    cond = jnp.zeros((dm.ncond, dm.d), F32).at[0].set(c_ctx).at[1:1 + dm.db].set(c)
    mod4 = _ada_mod(dm, cond, w_ada, b_ada).reshape(dm.depth, dm.ncond, 1, 6 * dm.d)
    pos = _grid_pos_embed(dm.dseq // GRID_W, dm.d)
    x, h = _embed(dm, x_prompt.reshape(dm.tp, dm.d), x_sample.reshape(dm.t - dm.tp, dm.d), pos, mod4)

    rw_pad = jnp.pad(router_w, ((0, 0), (0, -router_w.shape[1] % LANES)))
    h0_ssm = state_ssm.reshape(dm.db, dm.depth, 2, SSM_GROUPS, dm.gw, dm.state)
    cl_ctx, cl_lat = min(SCAN_CHUNK, dm.seq), min(SCAN_CHUNK, dm.dseq)
    ssm_states, gdn_states = [], []
    for l in range(dm.depth):
        proj = _in_proj(dm, h, w_in, l)
        dt, g, beta = _gates(dm, proj[:, dm.off_dt:dm.off_dt + 2 * dm.heads],
                             proj[:, dm.off_a:dm.off_a + 2 * dm.vh], proj[:, dm.off_b:dm.off_b + 2 * dm.vh],
                             ssm_dt_bias[l], gdn_dt_bias[l], gdn_a_log[l])
        act_s = _conv_act(dm, proj, dm.off_xbc, dm.xbc, conv_ssm_w, conv_ssm_b, l)
        dtr, alr = _ssd_gate_layouts(dm, dt, ssm_a_log[l])
        dskip = jnp.repeat(ssm_d[l].astype(F32), SSM_HEADDIM).reshape(1, dm.inner)
        nw_s = ssm_norm_w[l].reshape(1, dm.inner)
        y_ssm, hs = _ssd_call(dm, act_s, proj, dtr, alr, dskip, nw_s, None, None, l, 0, dm.nb, dm.seq)
        (y_ssm,) = _ssd_call(dm, act_s, proj, dtr, alr, dskip, nw_s, y_ssm, h0_ssm, l, dm.tp, dm.db, dm.dseq)
        ssm_states.append(hs.reshape(dm.nb, 2, dm.heads, SSM_HEADDIM, dm.state))
        grow_ctx, grow_lat = _gdn_gate_layouts(dm, g, beta, [(0, dm.tp, cl_ctx), (dm.tp, dm.t - dm.tp, cl_lat)])
        nw_g = gdn_norm_w[l].reshape(1, GDN_DV)
        y_gdn, ss = _gdn_call(dm, proj, conv_gdn_w, grow_ctx, nw_g, None, None, l, 0, dm.nb, dm.seq)
        (y_gdn,) = _gdn_call(dm, proj, conv_gdn_w, grow_lat, nw_g, y_gdn, state_gdn, l, dm.tp, dm.db, dm.dseq)
        gdn_states.append(ss)
        merged = _branch_merge(dm, y_ssm, y_gdn, w_branch_ssm, w_branch_gdn, proj, l)
        pre = _out_proj(dm, merged, w_out, x, mod4, l, alpha)
        x, h2, logits = _ln_router(dm, pre, ln1_g, ln1_b, mod4, rw_pad, l)
        yb, slot_dest, sel = _moe(dm, h2, logits, router_b, w_gate, w_up, w_down, l)
        outs = _ln_res(dm, x, yb, slot_dest, sel, ln2_g, ln2_b, mod4, l, alpha)
        x = outs[0]
        if l + 1 < dm.depth:
            h = outs[1]

    y_prompt = x[:dm.tp].reshape(dm.nb, dm.seq, dm.d)
    y_sample = x[dm.tp:].reshape(dm.db, dm.dseq, dm.d)
    return (y_prompt, y_sample, jnp.stack(ssm_states, axis=1), jnp.stack(gdn_states, axis=1))
```

```python
import functools
import math
from typing import NamedTuple

import jax
import jax.numpy as jnp
from jax import lax
from jax.experimental import pallas as pl
from jax.experimental.pallas import tpu as pltpu

F32 = jnp.float32
BF16 = jnp.bfloat16
HI = lax.Precision.HIGHEST

CONV_K = 5
SSM_HEADDIM = 64
SSM_GROUPS = 8
GDN_DK = 128
GDN_DV = 128
ROUTER_GROUPS = 8
TOP_K = 2
GRID_W = 64
POS_BASE = 10000.0
LN_EPS = 1e-5
RMS_EPS = 1e-6

LANES = 128
SUBLANES = 8
VMEM_LIMIT = 56 * 1024 * 1024

SCAN_CHUNK = 128
SSD_HEADS_PER_STAGE = 16
MOE_ROWS = 128


class Dims(NamedTuple):
    d: int
    depth: int
    nb: int
    seq: int
    db: int
    dseq: int
    tp: int
    t: int
    inner: int
    heads: int
    hpg: int
    gw: int
    state: int
    xbc: int
    qkh: int
    vh: int
    qkv: int
    val: int
    n_exp: int
    dff: int
    in_cols: int
    off_xbc: int
    off_dt: int
    off_qkv: int
    off_zg: int
    off_a: int
    off_b: int
    off_ga: int
    off_gb: int
    ncond: int


def _dims(x_prompt, x_sample, state_ssm, w_in, w_gate):
    nb, seq, d = x_prompt.shape
    db, dseq, _ = x_sample.shape
    depth = w_in.shape[0]
    inner = 2 * d
    heads = inner // SSM_HEADDIM
    state = state_ssm.shape[-1]
    xbc = inner + 2 * SSM_GROUPS * state
    qkh = d // GDN_DK
    vh = 2 * qkh
    key_dim = qkh * GDN_DK
    val = vh * GDN_DV
    qkv = 2 * key_dim + val
    off_xbc = inner
    off_dt = off_xbc + xbc
    off_qkv = off_dt + 2 * heads
    off_zg = off_qkv + qkv
    off_a = off_zg + val
    off_b = off_a + 2 * vh
    off_ga = off_b + 2 * vh
    off_gb = off_ga + d
    in_cols = off_gb + d
    assert in_cols == w_in.shape[2]
    ncond = -(-(1 + db) // SUBLANES) * SUBLANES
    return Dims(d=d, depth=depth, nb=nb, seq=seq, db=db, dseq=dseq, tp=nb * seq, t=nb * seq + db * dseq,
                inner=inner, heads=heads, hpg=heads // SSM_GROUPS, gw=inner // SSM_GROUPS, state=state, xbc=xbc,
                qkh=qkh, vh=vh, qkv=qkv, val=val, n_exp=w_gate.shape[1], dff=w_gate.shape[3], in_cols=in_cols,
                off_xbc=off_xbc, off_dt=off_dt, off_qkv=off_qkv, off_zg=off_zg, off_a=off_a, off_b=off_b,
                off_ga=off_ga, off_gb=off_gb, ncond=ncond)


def _tile(pref, *sizes):
    g = 0
    for s in sizes:
        g = math.gcd(g, s)
    t = math.gcd(pref, g)
    return t


def _params(sem, vmem=VMEM_LIMIT):
    return pltpu.CompilerParams(dimension_semantics=sem, vmem_limit_bytes=vmem)


def _sigmoid(x):
    return 1.0 / (1.0 + jnp.exp(-x))


def _silu(x):
    return x * _sigmoid(x)


def _softplus(x):
    return jnp.maximum(x, 0.0) + jnp.log(1.0 + jnp.exp(-jnp.abs(x)))


def _cond_of_row(dm, row0):
    return jnp.where(row0 < dm.tp, 0, 1 + (row0 - dm.tp) // dm.dseq)


def _ada_kernel(c_ref, w_ref, b_ref, o_ref):
    s = _silu(c_ref[...]).astype(BF16)
    o_ref[...] = jnp.dot(s, w_ref[...].astype(BF16), preferred_element_type=F32) + b_ref[...]


def _ada_mod(dm, cond, w_ada, b_ada):
    n = w_ada.shape[2]
    tn = _tile(512, n)
    return pl.pallas_call(
        _ada_kernel,
        grid=(dm.depth, n // tn),
        in_specs=[pl.BlockSpec((dm.ncond, dm.d), lambda l, j: (0, 0)),
                  pl.BlockSpec((None, dm.d, tn), lambda l, j: (l, 0, j)),
                  pl.BlockSpec((None, 1, tn), lambda l, j: (l, 0, j))],
        out_specs=pl.BlockSpec((None, dm.ncond, tn), lambda l, j: (l, 0, j)),
        out_shape=jax.ShapeDtypeStruct((dm.depth, dm.ncond, n), F32),
        compiler_params=_params(("arbitrary", "arbitrary")),
        name="ada_mod",
    )(cond, w_ada, b_ada.reshape(dm.depth, 1, n))


def _mod_spec(dm, tm, l, which, width=None, col_of=None):
    width = dm.d if width is None else width
    per = dm.d // width

    def idx(*g):
        i, j = (g[0], 0) if col_of is None else col_of(*g)
        return (l, _cond_of_row(dm, i * tm), 0, which * per + j)

    return pl.BlockSpec((None, None, 1, width), idx)


def _embed_kernel(xp_ref, xs_ref, pos_ref, sh_ref, sc_ref, x_ref, h_ref, *, npb):
    is_ctx = pl.program_id(0) < npb
    x = jnp.where(is_ctx, xp_ref[...], xs_ref[...] + pos_ref[...])
    x_ref[...] = x
    h_ref[...] = (x * (1.0 + sc_ref[...]) + sh_ref[...]).astype(BF16)


def _embed(dm, xp, xs, pos, mod4):
    tm = _tile(256, dm.seq, dm.dseq)
    npb = dm.tp // tm
    nsb = (dm.t - dm.tp) // tm
    ppb = dm.dseq // tm
    row = lambda i: (i, 0)
    return pl.pallas_call(
        functools.partial(_embed_kernel, npb=npb),
        grid=(dm.t // tm,),
        in_specs=[pl.BlockSpec((tm, dm.d), lambda i: (jnp.minimum(i, npb - 1), 0)),
                  pl.BlockSpec((tm, dm.d), lambda i: (jnp.clip(i - npb, 0, nsb - 1), 0)),
                  pl.BlockSpec((tm, dm.d), lambda i: (jnp.maximum(i - npb, 0) % ppb, 0)),
                  _mod_spec(dm, tm, 0, 0), _mod_spec(dm, tm, 0, 1)],
        out_specs=[pl.BlockSpec((tm, dm.d), row), pl.BlockSpec((tm, dm.d), row)],
        out_shape=[jax.ShapeDtypeStruct((dm.t, dm.d), F32), jax.ShapeDtypeStruct((dm.t, dm.d), BF16)],
        compiler_params=_params(("arbitrary",)),
        name="embed_mod",
    )(xp, xs, pos, mod4, mod4)


def _mm_res_kernel(x_ref, w_ref, r_ref, g_ref, o_ref, wb_ref, *, alpha):
    @pl.when(pl.program_id(1) == 0)
    def _():
        wb_ref[...] = w_ref[...].astype(BF16)

    acc = jnp.dot(x_ref[...], wb_ref[...], preferred_element_type=F32)
    o_ref[...] = alpha * r_ref[...] + g_ref[...] * acc


def _mm_stream_kernel(x_ref, w_ref, o_ref):
    o_ref[...] = jnp.dot(x_ref[...], w_ref[...].astype(BF16), preferred_element_type=F32).astype(o_ref.dtype)


def _in_proj(dm, h, w_in, l):
    k = dm.d
    n = dm.in_cols
    tm = _tile(2048, dm.t)
    tn = _tile(256, n)
    return pl.pallas_call(
        _mm_stream_kernel,
        grid=(dm.t // tm, n // tn),
        in_specs=[pl.BlockSpec((tm, k), lambda i, j: (i, 0)),
                  pl.BlockSpec((None, k, tn), lambda i, j: (l, 0, j))],
        out_specs=pl.BlockSpec((tm, tn), lambda i, j: (i, j)),
        out_shape=jax.ShapeDtypeStruct((dm.t, n), F32),
        compiler_params=_params(("arbitrary", "arbitrary")),
        name="in_proj",
    )(h, w_in)


def _out_proj(dm, merged, w_out, x, mod4, l, alpha):
    k = dm.d
    tm = _tile(1024, dm.tp, dm.dseq)
    tn = _tile(512, dm.d)
    return pl.pallas_call(
        functools.partial(_mm_res_kernel, alpha=alpha),
        grid=(dm.d // tn, dm.t // tm),
        in_specs=[pl.BlockSpec((tm, k), lambda j, i: (i, 0)),
                  pl.BlockSpec((None, k, tn), lambda j, i: (l, 0, j)),
                  pl.BlockSpec((tm, tn), lambda j, i: (i, j)),
                  _mod_spec(dm, tm, l, 2, width=tn, col_of=lambda j, i: (i, j))],
        out_specs=pl.BlockSpec((tm, tn), lambda j, i: (i, j)),
        out_shape=jax.ShapeDtypeStruct((dm.t, dm.d), F32),
        scratch_shapes=[pltpu.VMEM((k, tn), BF16)],
        compiler_params=_params(("arbitrary", "arbitrary")),
        name="out_proj",
    )(merged, w_out, x, mod4)


def _branch_kernel(ya_ref, yb_ref, wa_ref, wb_ref, ga_ref, gb_ref, o_ref, acc_ref, res_ref, wpa_ref, wpb_ref,
                   *, nk, tk):
    i = pl.program_id(1)
    kb = pl.program_id(2)

    @pl.when(kb % nk == 0)
    def _():
        acc_ref[...] = jnp.zeros_like(acc_ref)

    def accumulate(y_ref, w_ref, wp_ref, kk):
        rows = pl.ds(pl.multiple_of(kk * tk, tk), tk)

        @pl.when(i == 0)
        def _():
            wp_ref[rows, :] = w_ref[...].astype(BF16)

        acc_ref[...] += jnp.dot(y_ref[...], wp_ref[rows, :], preferred_element_type=F32)

    @pl.when(kb < nk)
    def _():
        accumulate(ya_ref, wa_ref, wpa_ref, kb)

    @pl.when(kb >= nk)
    def _():
        accumulate(yb_ref, wb_ref, wpb_ref, kb - nk)

    @pl.when(kb == nk - 1)
    def _():
        res_ref[...] = _sigmoid(ga_ref[...]) * acc_ref[...]

    @pl.when(kb == 2 * nk - 1)
    def _():
        o_ref[...] = (res_ref[...] + _sigmoid(gb_ref[...]) * acc_ref[...]).astype(o_ref.dtype)


def _branch_merge(dm, y_ssm, y_gdn, w_a, w_b, proj, l):
    assert dm.inner == dm.val
    kdim = dm.inner
    tm = _tile(1024, dm.t)
    tn = _tile(512, dm.d, dm.off_ga, dm.off_gb)
    tk = _tile(1024, kdim)
    nk = kdim // tk
    ca, cb = dm.off_ga // tn, dm.off_gb // tn

    def wa_idx(j, i, kb):
        return (l, jnp.where(i == 0, jnp.minimum(kb, nk - 1), nk - 1), j)

    def wb_idx(j, i, kb):
        return (l, jnp.where(i == 0, jnp.maximum(kb - nk, 0), nk - 1), j)

    return pl.pallas_call(
        functools.partial(_branch_kernel, nk=nk, tk=tk),
        grid=(dm.d // tn, dm.t // tm, 2 * nk),
        in_specs=[pl.BlockSpec((tm, tk), lambda j, i, kb: (i, jnp.minimum(kb, nk - 1))),
                  pl.BlockSpec((tm, tk), lambda j, i, kb: (i, jnp.maximum(kb - nk, 0))),
                  pl.BlockSpec((None, tk, tn), wa_idx),
                  pl.BlockSpec((None, tk, tn), wb_idx),
                  pl.BlockSpec((tm, tn), lambda j, i, kb: (i, ca + j)),
                  pl.BlockSpec((tm, tn), lambda j, i, kb: (i, cb + j))],
        out_specs=pl.BlockSpec((tm, tn), lambda j, i, kb: (i, j)),
        out_shape=jax.ShapeDtypeStruct((dm.t, dm.d), BF16),
        scratch_shapes=[pltpu.VMEM((tm, tn), F32), pltpu.VMEM((tm, tn), F32),
                        pltpu.VMEM((kdim, tn), BF16), pltpu.VMEM((kdim, tn), BF16)],
        compiler_params=_params(("arbitrary", "arbitrary", "arbitrary")),
        name="branch_merge",
    )(y_ssm, y_gdn, w_a, w_b, proj, proj)


def _gates_kernel(dtr_ref, ar_ref, br_ref, dtb_ref, gb_ref, al_ref, dt_ref, g_ref, beta_ref):
    dt_ref[...] = _softplus(dtr_ref[...] + dtb_ref[...])
    g_ref[...] = -jnp.exp(al_ref[...]) * _softplus(ar_ref[...] + gb_ref[...])
    beta_ref[...] = _sigmoid(br_ref[...])


def _gates(dm, dt_raw, a_raw, b_raw, dt_bias, g_bias, a_log):
    tm = _tile(512, dm.t)
    w1, w2 = 2 * dm.heads, 2 * dm.vh
    row = lambda i: (i, 0)
    fix = lambda i: (0, 0)
    return pl.pallas_call(
        _gates_kernel,
        grid=(dm.t // tm,),
        in_specs=[pl.BlockSpec((tm, w1), row), pl.BlockSpec((tm, w2), row), pl.BlockSpec((tm, w2), row),
                  pl.BlockSpec((1, w1), fix), pl.BlockSpec((1, w2), fix), pl.BlockSpec((1, w2), fix)],
        out_specs=[pl.BlockSpec((tm, w1), row), pl.BlockSpec((tm, w2), row), pl.BlockSpec((tm, w2), row)],
        out_shape=[jax.ShapeDtypeStruct((dm.t, w1), F32), jax.ShapeDtypeStruct((dm.t, w2), F32),
                   jax.ShapeDtypeStruct((dm.t, w2), F32)],
        compiler_params=_params(("arbitrary",)),
        name="gate_prep",
    )(dt_raw, a_raw, b_raw, dt_bias.reshape(1, w1), g_bias.reshape(1, w2), a_log.reshape(1, w2))


def _conv_kernel(*refs, rows, blocks_ctx, blocks_per_ctx_seq, blocks_per_lat_seq, has_bias):
    if has_bias:
        x_ref, prev_ref, next_ref, w_ref, b_ref, o_ref = refs
    else:
        x_ref, prev_ref, next_ref, w_ref, o_ref = refs
        b_ref = None
    i = pl.program_id(0)
    in_ctx = i < blocks_ctx
    pos = jnp.where(in_ctx, i % blocks_per_ctx_seq, (i - blocks_ctx) % blocks_per_lat_seq)
    per = jnp.where(in_ctx, blocks_per_ctx_seq, blocks_per_lat_seq)
    prev = jnp.where(pos == 0, 0.0, prev_ref[...])
    nxt = jnp.where(pos == per - 1, 0.0, next_ref[...])
    ext = jnp.concatenate([prev, x_ref[...], nxt], axis=0)
    n = rows + 2 * SUBLANES
    pad = CONV_K // 2
    w = w_ref[...]
    acc = None
    for tap in range(CONV_K):
        start = SUBLANES - pad + tap
        shifted = ext if start == 0 else pltpu.roll(ext, n - start, 0)
        term = shifted[:rows] * w[tap:tap + 1, :]
        acc = term if acc is None else acc + term
    if has_bias:
        acc = acc + b_ref[...]
    o_ref[...] = _silu(acc)


def _conv_act(dm, proj, col_off, width, w, b, l):
    rows = _tile(256, dm.seq, dm.dseq)
    tc = _tile(2048, col_off, width)
    c0 = col_off // tc
    hb = rows // SUBLANES
    nhalo = dm.t // SUBLANES
    has_bias = b is not None
    in_specs = [pl.BlockSpec((rows, tc), lambda i, j: (i, c0 + j)),
                pl.BlockSpec((SUBLANES, tc), lambda i, j: (jnp.maximum(i * hb - 1, 0), c0 + j)),
                pl.BlockSpec((SUBLANES, tc), lambda i, j: (jnp.minimum((i + 1) * hb, nhalo - 1), c0 + j)),
                pl.BlockSpec((None, CONV_K, tc), lambda i, j: (l, 0, j))]
    args = [proj, proj, proj, w]
    if has_bias:
        in_specs.append(pl.BlockSpec((None, 1, tc), lambda i, j: (l, 0, j)))
        args.append(b.reshape(dm.depth, 1, width))
    return pl.pallas_call(
        functools.partial(_conv_kernel, rows=rows, blocks_ctx=dm.tp // rows, blocks_per_ctx_seq=dm.seq // rows,
                          blocks_per_lat_seq=dm.dseq // rows, has_bias=has_bias),
        grid=(dm.t // rows, width // tc),
        in_specs=in_specs,
        out_specs=pl.BlockSpec((rows, tc), lambda i, j: (i, j)),
        out_shape=jax.ShapeDtypeStruct((dm.t, width), F32),
        compiler_params=_params(("arbitrary", "arbitrary")),
        name="conv_act",
    )(*args)


def _ssd_kernel(*refs, nc, hpg, hd, has_state, want_state):
    it = iter(refs)
    xs_ref, b_ref, c_ref, dtr_ref, alr_ref, z_ref, dsk_ref, nw_ref = (next(it) for _ in range(8))
    h0_ref = next(it) if has_state else None
    y_ref = next(it)
    hout_ref = next(it) if want_state else None
    ht_ref, yf_ref, xw_ref, yc_ref = (next(it) for _ in range(4))

    d = pl.program_id(2)
    c = pl.program_id(3)
    q = xs_ref.shape[0]
    fwd = d == 0
    ii = lax.broadcasted_iota(jnp.int32, (q, q), 0)
    jj = lax.broadcasted_iota(jnp.int32, (q, q), 1)
    mask = ((ii - jj) * (1 - 2 * d)) >= 0
    tri = mask.astype(F32)

    dt_r = dtr_ref[...]
    a_r = dt_r * (-jnp.exp(alr_ref[...]))
    acum_r = lax.dot_general(a_r, tri, (((1,), (1,)), ((), ())), precision=HI, preferred_element_type=F32)
    alast_r = jnp.where(fwd, acum_r[:, q - 1:q], acum_r[:, 0:1])
    assert q == LANES and 2 * hpg <= LANES
    cols = jnp.concatenate([acum_r, dt_r, jnp.zeros((LANES - 2 * hpg, q), F32)], axis=0).T
    acum_c = cols[:, :hpg]
    dt_c = cols[:, hpg:2 * hpg]
    alast_c = jnp.where(fwd, acum_c[q - 1:q, :], acum_c[0:1, :])

    bm = b_ref[...].astype(BF16)
    cm = c_ref[...].astype(BF16)
    cb = lax.dot_general(cm, bm, (((1,), (1,)), ((), ())), preferred_element_type=F32)

    @pl.when(c == 0)
    def _():
        if has_state:
            ht_ref[...] = h0_ref[...].T
        else:
            ht_ref[...] = jnp.zeros_like(ht_ref)

    ht = ht_ref[...]
    yo = jnp.dot(cm, ht.astype(BF16), preferred_element_type=F32)
    xs = xs_ref[...]
    log_end = alast_c - acum_c + jnp.log(dt_c)
    for r0 in range(0, hpg, SSD_HEADS_PER_STAGE):
        heads = range(r0, min(r0 + SSD_HEADS_PER_STAGE, hpg))
        sls = [slice(r * hd, (r + 1) * hd) for r in heads]
        acs = [jnp.broadcast_to(acum_c[:, r:r + 1], (q, q)) for r in heads]
        les = [jnp.broadcast_to(log_end[:, r:r + 1], (q, hd)) for r in heads]
        decs = [jnp.where(mask, jnp.exp(ac - acum_r[r:r + 1, :]), 0.0) for r, ac in zip(heads, acs)]
        wgts = [(cb * dec * dt_r[r:r + 1, :]).astype(BF16) for r, dec in zip(heads, decs)]
        xrs = [xs[:, sl] for sl in sls]
        yds = [jnp.dot(wgt, x_r.astype(BF16), preferred_element_type=F32) for wgt, x_r in zip(wgts, xrs)]
        for sl, ac, le, x_r, yd in zip(sls, acs, les, xrs, yds):
            yc_ref[:, sl] = yd + yo[:, sl] * jnp.exp(ac[:, :hd])
            xw_ref[:, sl] = (x_r * jnp.exp(le)).astype(BF16)
    st = lax.dot_general(bm, xw_ref[...], (((0,), (0,)), ((), ())), preferred_element_type=F32)
    for r in range(hpg):
        sl = slice(r * hd, (r + 1) * hd)
        ht_ref[:, sl] = ht[:, sl] * jnp.exp(alast_r[r:r + 1, :]) + st[:, sl]

    cidx = c + d * (nc - 1 - 2 * c)
    rows = pl.ds(pl.multiple_of(cidx * q, q), q)

    @pl.when(fwd)
    def _():
        yf_ref[rows, :] = yc_ref[...]

    @pl.when(d == 1)
    def _():
        tot = yf_ref[rows, :] + yc_ref[...] + dsk_ref[...] * xs
        gated = tot * _silu(z_ref[...])
        ms = jnp.mean(gated * gated, axis=-1, keepdims=True)
        y_ref[...] = (gated * lax.rsqrt(ms + RMS_EPS) * nw_ref[...]).astype(y_ref.dtype)

    if want_state:
        @pl.when(c == nc - 1)
        def _():
            hout_ref[...] = ht_ref[...].T


def _ssd_call(dm, act, proj, dtr, alr, dskip, nw, y_prev, h0, l, row_off, nseq, length):
    q = min(SCAN_CHUNK, length)
    nc = length // q
    base = row_off // q
    g_n, gw, n = SSM_GROUPS, dm.gw, dm.state
    has_state = h0 is not None
    want_state = not has_state
    bcol = dm.inner // n

    def rb(b, d, c):
        return base + b * nc + c + d * (nc - 1 - 2 * c)

    def rb_out(b, d, c):
        return base + b * nc + jnp.where(d == 0, nc - 1, nc - 1 - c)

    in_specs = [pl.BlockSpec((q, gw), lambda b, g, d, c: (rb(b, d, c), g)),
                pl.BlockSpec((q, n), lambda b, g, d, c: (rb(b, d, c), bcol + g)),
                pl.BlockSpec((q, n), lambda b, g, d, c: (rb(b, d, c), bcol + g_n + g)),
                pl.BlockSpec((None, None, dm.hpg, q), lambda b, g, d, c: (d, g, 0, rb(b, d, c))),
                pl.BlockSpec((None, None, dm.hpg, 1), lambda b, g, d, c: (d, g, 0, 0)),
                pl.BlockSpec((q, gw), lambda b, g, d, c: (rb_out(b, d, c), g)),
                pl.BlockSpec((1, gw), lambda b, g, d, c: (0, g)),
                pl.BlockSpec((1, gw), lambda b, g, d, c: (0, g))]
    args = [act, act, act, dtr, alr, proj, dskip, nw]
    if has_state:
        in_specs.append(pl.BlockSpec((None, None, None, None, gw, n), lambda b, g, d, c: (b, l, d, g, 0, 0)))
        args.append(h0)
    out_specs = [pl.BlockSpec((q, gw), lambda b, g, d, c: (rb_out(b, d, c), g))]
    out_shape = [jax.ShapeDtypeStruct((dm.t, dm.inner), BF16)]
    if want_state:
        out_specs.append(pl.BlockSpec((None, None, None, gw, n), lambda b, g, d, c: (b, d, g, 0, 0)))
        out_shape.append(jax.ShapeDtypeStruct((nseq, 2, g_n, gw, n), F32))
    aliases = {}
    if y_prev is not None:
        in_specs.append(pl.BlockSpec(memory_space=pl.ANY))
        args.append(y_prev)
        aliases = {len(args) - 1: 0}

    def body(*refs):
        refs = list(refs)
        if y_prev is not None:
            n_in = len(args)
            del refs[n_in - 1]
        _ssd_kernel(*refs, nc=nc, hpg=dm.hpg, hd=SSM_HEADDIM, has_state=has_state, want_state=want_state)

    return pl.pallas_call(
        body,
        grid=(nseq, g_n, 2, nc),
        in_specs=in_specs,
        out_specs=out_specs,
        out_shape=out_shape,
        scratch_shapes=[pltpu.VMEM((n, gw), F32), pltpu.VMEM((length, gw), F32),
                        pltpu.VMEM((q, gw), BF16), pltpu.VMEM((q, gw), F32)],
        input_output_aliases=aliases,
        compiler_params=_params(("arbitrary",) * 4),
        name="ssd_scan",
    )(*args)


def _conv_silu_seq(x, w):
    n = x.shape[0]
    pad = CONV_K // 2
    t = lax.broadcasted_iota(jnp.int32, x.shape, 0)
    acc = x * w[pad:pad + 1, :]
    for tap in range(CONV_K):
        off = tap - pad
        if off == 0:
            continue
        shifted = pltpu.roll(x, (-off) % n, 0)
        inside = (t >= -off) if off < 0 else (t < n - off)
        acc = acc + jnp.where(inside, shifted, 0.0) * w[tap:tap + 1, :]
    return _silu(acc)


def _gdn_kernel(*refs, nc, cl, has_state, want_state):
    it = iter(refs)
    q_ref, k_ref, v0_ref, v1_ref, wq_ref, wk_ref, wv0_ref, wv1_ref = (next(it) for _ in range(8))
    z0_ref, z1_ref, grow_ref, nw_ref = (next(it) for _ in range(4))
    s0_ref = next(it) if has_state else None
    y_ref = next(it)
    sout_ref = next(it) if want_state else None
    qn_ref, kn_ref, va_ref, t_ref, qk_ref, gc_ref, o_ref, s_ref = (next(it) for _ in range(8))
    z_refs = (z0_ref, z1_ref)
    dk = q_ref.shape[1]
    dv = v0_ref.shape[1]

    qv = _conv_silu_seq(q_ref[...], wq_ref[...])
    qn_ref[...] = (qv * lax.rsqrt(jnp.sum(qv * qv, -1, keepdims=True) + RMS_EPS) * (dk ** -0.5)).astype(BF16)
    kv = _conv_silu_seq(k_ref[...], wk_ref[...])
    kn_ref[...] = (kv * lax.rsqrt(jnp.sum(kv * kv, -1, keepdims=True) + RMS_EPS)).astype(BF16)
    va_ref[0] = _conv_silu_seq(v0_ref[...], wv0_ref[...])
    va_ref[1] = _conv_silu_seq(v1_ref[...], wv1_ref[...])

    ii = lax.broadcasted_iota(jnp.int32, (cl, cl), 0)
    jj = lax.broadcasted_iota(jnp.int32, (cl, cl), 1)
    eye = (ii == jj).astype(F32)
    tri_f = (ii >= jj).astype(F32)
    incl = (ii >= jj, ii <= jj)
    strict = (ii > jj, ii < jj)
    assert cl == LANES
    sub = lax.broadcasted_iota(jnp.int32, (SUBLANES, cl), 0)
    n_lvl = int(math.log2(cl))
    pair_mask = [((ii >> (k + 1)) == (jj >> (k + 1))) & ((ii >> k) != (jj >> k)) for k in range(n_lvl)]
    nt = (((1,), (1,)), ((), ()))

    ua = 4 if nc % 4 == 0 else (2 if nc % 2 == 0 else 1)

    def phase_a(t, carry):
        nms, tms, slots = [], [], []
        for u in range(ua):
            ci = t * ua + u
            rows = pl.ds(pl.multiple_of(ci * cl, cl), cl)
            kc = kn_ref[rows, :]
            qc = qn_ref[rows, :]
            kk = lax.dot_general(kc, kc, nt, preferred_element_type=F32)
            qk = lax.dot_general(qc, kc, nt, preferred_element_type=F32)
            grow = grow_ref[ci]
            pre_r = lax.dot_general(grow, tri_f, nt, precision=HI, preferred_element_type=F32)
            suf_r = jnp.sum(grow, axis=1, keepdims=True) - pre_r + grow
            grb = jnp.where(sub >= 4, grow, jnp.where(sub >= 2, suf_r, pre_r))
            gcb = jnp.concatenate([grb, jnp.zeros((LANES - SUBLANES, cl), F32)], axis=0).T
            gc_ref[rows, :] = gcb
            for kx in range(4):
                dd = kx // 2
                gi = gcb[:, kx:kx + 1]
                gj = grb[kx:kx + 1, :]
                bi = gcb[:, 4 + kx:5 + kx]
                dec = jnp.where(incl[dd], jnp.exp(gi - gj), 0.0)
                nm = jnp.where(strict[dd], bi * kk * dec, 0.0)
                qk_ref[ci, kx] = (qk * dec).astype(BF16)
                nms.append(nm)
                tms.append(eye - jnp.where(pair_mask[0], nm, 0.0))
                slots.append((ci, kx))
        for lvl in range(1, n_lvl):
            tbs = [tm.astype(BF16) for tm in tms]
            cts = [jnp.dot(jnp.where(pair_mask[lvl], nm, 0.0).astype(BF16), tb, preferred_element_type=F32)
                   for nm, tb in zip(nms, tbs)]
            tms = [tm - jnp.dot(tb, ct.astype(BF16), preferred_element_type=F32)
                   for tm, tb, ct in zip(tms, tbs, cts)]
        for (ci, kx), tm in zip(slots, tms):
            t_ref[ci, kx] = tm.astype(BF16)
        return carry

    lax.fori_loop(0, nc // ua, phase_a, 0)

    for kx in range(4):
        if has_state:
            s_ref[kx] = s0_ref[kx // 2, kx % 2]
        else:
            s_ref[kx] = jnp.zeros((dk, dv), F32)

    def phase_b(t, carry):
        cidx = (t, t, nc - 1 - t, nc - 1 - t)
        rows = [pl.ds(pl.multiple_of(ci * cl, cl), cl) for ci in cidx]
        kcs = [kn_ref[r, :] for r in rows]
        ss = [s_ref[kx] for kx in range(4)]
        gis = [gc_ref[rows[kx], :][:, kx:kx + 1] for kx in range(4)]
        bis = [gc_ref[rows[kx], :][:, 4 + kx:5 + kx] for kx in range(4)]
        egis = [jnp.exp(gi) for gi in gis]
        kqs = [jnp.dot(jnp.concatenate([kcs[kx], qn_ref[rows[kx], :]], axis=0), ss[kx].astype(BF16),
                       preferred_element_type=F32) for kx in range(4)]
        rhs = [(bis[kx] * (va_ref[kx % 2, rows[kx], :] - egis[kx] * kqs[kx][:cl])).astype(BF16) for kx in range(4)]
        vnews = [jnp.dot(t_ref[cidx[kx], kx], rhs[kx], preferred_element_type=F32) for kx in range(4)]
        for kx in range(4):
            o_ref[kx // 2, kx % 2, rows[kx], :] = egis[kx] * kqs[kx][cl:] + jnp.dot(
                qk_ref[cidx[kx], kx], vnews[kx].astype(BF16), preferred_element_type=F32)
        glasts = [gis[kx][cl - 1:cl, :] if kx < 2 else gis[kx][0:1, :] for kx in range(4)]
        vss = [(jnp.exp(glasts[kx] - gis[kx]) * vnews[kx]).astype(BF16) for kx in range(4)]
        for kx in range(4):
            s_ref[kx] = ss[kx] * jnp.exp(glasts[kx]) + lax.dot_general(
                kcs[kx], vss[kx], (((0,), (0,)), ((), ())), preferred_element_type=F32)
        return carry

    lax.fori_loop(0, nc, phase_b, 0)

    for e in range(2):
        o = o_ref[0, e] + o_ref[1, e]
        on = o * lax.rsqrt(jnp.mean(o * o, -1, keepdims=True) + RMS_EPS)
        y_ref[:, e * dv:(e + 1) * dv] = (on * nw_ref[...] * _silu(z_refs[e][...])).astype(y_ref.dtype)
    if want_state:
        for kx in range(4):
            sout_ref[kx // 2, kx % 2] = s_ref[kx]


def _gdn_call(dm, proj, conv_w, grow, nw, y_prev, s0, l, row_off, nseq, length):
    cl = min(SCAN_CHUNK, length)
    nc = length // cl
    assert row_off % length == 0
    base = row_off // length
    dk, dv = GDN_DK, GDN_DV
    has_state = s0 is not None
    want_state = not has_state
    assert dm.off_qkv % dk == 0 and dm.off_zg % dv == 0
    qcol = dm.off_qkv // dk
    kcol = dm.qkh
    vcol = 2 * dm.qkh
    zcol = dm.off_zg // dv

    def col_specs(first, shape, row_of):
        return [pl.BlockSpec(shape, lambda b, j: row_of(b) + (first + j,)),
                pl.BlockSpec(shape, lambda b, j: row_of(b) + (first + kcol + j,)),
                pl.BlockSpec(shape, lambda b, j: row_of(b) + (first + vcol + 2 * j,)),
                pl.BlockSpec(shape, lambda b, j: row_of(b) + (first + vcol + 2 * j + 1,))]

    in_specs = (col_specs(qcol, (length, dk), lambda b: (base + b,))
                + col_specs(0, (None, CONV_K, dk), lambda b: (l, 0))
                + [pl.BlockSpec((length, dv), lambda b, j: (base + b, zcol + 2 * j)),
                   pl.BlockSpec((length, dv), lambda b, j: (base + b, zcol + 2 * j + 1)),
                   pl.BlockSpec((None, nc, SUBLANES, cl), lambda b, j: (j, b, 0, 0)),
                   pl.BlockSpec((1, dv), lambda b, j: (0, 0))])
    args = [proj] * 4 + [conv_w] * 4 + [proj, proj, grow, nw]
    if has_state:
        in_specs.append(pl.BlockSpec((None, None, 2, 2, dk, dv), lambda b, j: (b, l, 0, j, 0, 0)))
        args.append(s0)
    out_specs = [pl.BlockSpec((length, 2 * dv), lambda b, j: (base + b, j))]
    out_shape = [jax.ShapeDtypeStruct((dm.t, dm.val), BF16)]
    if want_state:
        out_specs.append(pl.BlockSpec((None, 2, 2, dk, dv), lambda b, j: (b, 0, j, 0, 0)))
        out_shape.append(jax.ShapeDtypeStruct((nseq, 2, dm.vh, dk, dv), F32))
    aliases = {}
    if y_prev is not None:
        in_specs.append(pl.BlockSpec(memory_space=pl.ANY))
        args.append(y_prev)
        aliases = {len(args) - 1: 0}

    def body(*refs):
        refs = list(refs)
        if y_prev is not None:
            del refs[len(args) - 1]
        _gdn_kernel(*refs, nc=nc, cl=cl, has_state=has_state, want_state=want_state)

    return pl.pallas_call(
        body,
        grid=(nseq, dm.qkh),
        in_specs=in_specs,
        out_specs=out_specs,
        out_shape=out_shape,
        scratch_shapes=[pltpu.VMEM((length, dk), BF16), pltpu.VMEM((length, dk), BF16),
                        pltpu.VMEM((2, length, dv), F32),
                        pltpu.VMEM((nc, 4, cl, cl), BF16), pltpu.VMEM((nc, 4, cl, cl), BF16),
                        pltpu.VMEM((length, LANES), F32),
                        pltpu.VMEM((2, 2, length, dv), F32), pltpu.VMEM((4, dk, dv), F32)],
        input_output_aliases=aliases,
        compiler_params=_params(("arbitrary", "arbitrary")),
        name="gdn_scan",
    )(*args)


def _layer_norm(x, g, b):
    mu = jnp.mean(x, -1, keepdims=True)
    xc = x - mu
    var = jnp.mean(xc * xc, -1, keepdims=True)
    return xc * lax.rsqrt(var + LN_EPS) * g + b


def _ln_router_kernel(pre_ref, g_ref, b_ref, sh_ref, sc_ref, rw_ref, x_ref, h_ref, lg_ref):
    x = _layer_norm(pre_ref[...], g_ref[...], b_ref[...])
    x_ref[...] = x
    h = x * (1.0 + sc_ref[...]) + sh_ref[...]
    h_ref[...] = h
    lg_ref[...] = jnp.dot(h, rw_ref[...], precision=HI, preferred_element_type=F32)


def _ln_router(dm, pre, g, b, mod4, rw_pad, l):
    tm = _tile(256, dm.seq, dm.dseq)
    row = lambda i: (i, 0)
    fix = lambda i: (0, 0)
    ne = rw_pad.shape[1]
    return pl.pallas_call(
        _ln_router_kernel,
        grid=(dm.t // tm,),
        in_specs=[pl.BlockSpec((tm, dm.d), row),
                  pl.BlockSpec((None, 1, dm.d), lambda i: (l, 0, 0)),
                  pl.BlockSpec((None, 1, dm.d), lambda i: (l, 0, 0)),
                  _mod_spec(dm, tm, l, 3), _mod_spec(dm, tm, l, 4),
                  pl.BlockSpec((dm.d, ne), fix)],
        out_specs=[pl.BlockSpec((tm, dm.d), row), pl.BlockSpec((tm, dm.d), row), pl.BlockSpec((tm, ne), row)],
        out_shape=[jax.ShapeDtypeStruct((dm.t, dm.d), F32), jax.ShapeDtypeStruct((dm.t, dm.d), F32),
                   jax.ShapeDtypeStruct((dm.t, ne), F32)],
        compiler_params=_params(("arbitrary",)),
        name="ln1_router",
    )(pre, g.reshape(dm.depth, 1, dm.d), b.reshape(dm.depth, 1, dm.d), mod4, mod4, rw_pad)


def _ln_res_kernel(*refs, alpha, with_mod, tm):
    if with_mod:
        slot_ref, x_ref, yb_ref, sel_ref, gt_ref, g_ref, b_ref, sh_ref, sc_ref, o_ref, h_ref, rows_ref, sem = refs
    else:
        slot_ref, x_ref, yb_ref, sel_ref, gt_ref, g_ref, b_ref, o_ref, rows_ref, sem = refs
    base = pl.program_id(0) * tm

    def copy(s):
        tok, k = s // TOP_K, s % TOP_K
        return _row_copy(yb_ref, slot_ref[(base + tok) * TOP_K + k], rows_ref.at[k], tok, sem)

    def start(s, carry):
        copy(s).start()
        return carry

    def wait(s, carry):
        copy(s).wait()
        return carry

    lax.fori_loop(0, tm * TOP_K, start, 0)
    lax.fori_loop(0, tm * TOP_K, wait, 0)
    sel = sel_ref[...]
    y = rows_ref[0] * sel[:, 2 * TOP_K:2 * TOP_K + 1]
    for k in range(1, TOP_K):
        y = y + rows_ref[k] * sel[:, 2 * TOP_K + k:2 * TOP_K + k + 1]
    x = _layer_norm(alpha * x_ref[...] + gt_ref[...] * y, g_ref[...], b_ref[...])
    o_ref[...] = x
    if with_mod:
        h_ref[...] = (x * (1.0 + sc_ref[...]) + sh_ref[...]).astype(BF16)


def _ln_res(dm, x, yb, slot_dest, sel, g, b, mod4, l, alpha):
    tm = _tile(256, dm.seq, dm.dseq)
    row = lambda i, s: (i, 0)
    with_mod = l + 1 < dm.depth
    ne = sel.shape[1]

    def mod_spec(layer, which):
        inner = _mod_spec(dm, tm, layer, which)
        return pl.BlockSpec(inner.block_shape, lambda i, s: inner.index_map(i))

    in_specs = [pl.BlockSpec((tm, dm.d), row), pl.BlockSpec(memory_space=pl.ANY), pl.BlockSpec((tm, ne), row),
                mod_spec(l, 5),
                pl.BlockSpec((None, 1, dm.d), lambda i, s: (l, 0, 0)),
                pl.BlockSpec((None, 1, dm.d), lambda i, s: (l, 0, 0))]
    args = [x, yb, sel, mod4, g.reshape(dm.depth, 1, dm.d), b.reshape(dm.depth, 1, dm.d)]
    out_specs = [pl.BlockSpec((tm, dm.d), row)]
    out_shape = [jax.ShapeDtypeStruct((dm.t, dm.d), F32)]
    if with_mod:
        in_specs += [mod_spec(l + 1, 0), mod_spec(l + 1, 1)]
        args += [mod4, mod4]
        out_specs.append(pl.BlockSpec((tm, dm.d), row))
        out_shape.append(jax.ShapeDtypeStruct((dm.t, dm.d), BF16))
    return pl.pallas_call(
        functools.partial(_ln_res_kernel, alpha=alpha, with_mod=with_mod, tm=tm),
        grid_spec=pltpu.PrefetchScalarGridSpec(
            num_scalar_prefetch=1,
            grid=(dm.t // tm,),
            in_specs=in_specs,
            out_specs=out_specs,
            scratch_shapes=[pltpu.VMEM((TOP_K, tm, dm.d), F32), pltpu.SemaphoreType.DMA(())]),
        out_shape=out_shape,
        compiler_params=_params(("arbitrary",)),
        name="ln2",
    )(slot_dest, *args)


def _ffn_up_kernel(e_ref, nu_ref, x_ref, wg_ref, wu_ref, h_ref, wgb_ref, wub_ref):
    i = pl.program_id(1)
    prev = e_ref[jnp.maximum(i - 1, 0)]

    @pl.when((i < nu_ref[0]) & ((i == 0) | (e_ref[i] != prev)))
    def _():
        wgb_ref[...] = wg_ref[...].astype(BF16)
        wub_ref[...] = wu_ref[...].astype(BF16)

    @pl.when(i < nu_ref[0])
    def _():
        x = x_ref[...].astype(BF16)
        a = jnp.dot(x, wgb_ref[...], preferred_element_type=F32)
        u = jnp.dot(x, wub_ref[...], preferred_element_type=F32)
        h_ref[...] = (_silu(a) * u).astype(h_ref.dtype)

    @pl.when(i >= nu_ref[0])
    def _():
        h_ref[...] = jnp.zeros_like(h_ref)


def _ffn_down_kernel(e_ref, nu_ref, h_ref, wd_ref, y_ref, wdb_ref):
    i = pl.program_id(1)
    prev = e_ref[jnp.maximum(i - 1, 0)]

    @pl.when((i < nu_ref[0]) & ((i == 0) | (e_ref[i] != prev)))
    def _():
        wdb_ref[...] = wd_ref[...].astype(BF16)

    @pl.when(i < nu_ref[0])
    def _():
        y_ref[...] = jnp.dot(h_ref[...], wdb_ref[...], preferred_element_type=F32)

    @pl.when(i >= nu_ref[0])
    def _():
        y_ref[...] = jnp.zeros_like(y_ref)


def _expert_ffn(dm, xb, blk_e, n_used, w_gate, w_up, w_down, l):
    n_buf = xb.shape[0]
    n_blk = n_buf // MOE_ROWS
    tf = _tile(512, dm.dff)
    tn = _tile(2048, dm.d)

    def blk(i, nu):
        return jnp.minimum(i, nu[0] - 1)

    hid = pl.pallas_call(
        _ffn_up_kernel,
        grid_spec=pltpu.PrefetchScalarGridSpec(
            num_scalar_prefetch=2,
            grid=(dm.dff // tf, n_blk),
            in_specs=[pl.BlockSpec((MOE_ROWS, dm.d), lambda f, i, e, nu: (blk(i, nu), 0)),
                      pl.BlockSpec((None, None, dm.d, tf), lambda f, i, e, nu: (l, e[i], 0, f)),
                      pl.BlockSpec((None, None, dm.d, tf), lambda f, i, e, nu: (l, e[i], 0, f))],
            out_specs=pl.BlockSpec((MOE_ROWS, tf), lambda f, i, e, nu: (i, f)),
            scratch_shapes=[pltpu.VMEM((dm.d, tf), BF16), pltpu.VMEM((dm.d, tf), BF16)]),
        out_shape=jax.ShapeDtypeStruct((n_buf, dm.dff), BF16),
        compiler_params=_params(("arbitrary", "arbitrary")),
        name="ffn_up",
    )(blk_e, n_used, xb, w_gate, w_up)

    return pl.pallas_call(
        _ffn_down_kernel,
        grid_spec=pltpu.PrefetchScalarGridSpec(
            num_scalar_prefetch=2,
            grid=(dm.d // tn, n_blk),
            in_specs=[pl.BlockSpec((MOE_ROWS, dm.dff), lambda j, i, e, nu: (blk(i, nu), 0)),
                      pl.BlockSpec((None, None, dm.dff, tn), lambda j, i, e, nu: (l, e[i], 0, j))],
            out_specs=pl.BlockSpec((MOE_ROWS, tn), lambda j, i, e, nu: (i, j)),
            scratch_shapes=[pltpu.VMEM((dm.dff, tn), BF16)]),
        out_shape=jax.ShapeDtypeStruct((n_buf, dm.d), F32),
        compiler_params=_params(("arbitrary", "arbitrary")),
        name="ffn_down",
    )(blk_e, n_used, hid, w_down)


def _route_kernel(lg_ref, rb_ref, o_ref, cnt_ref, carry_ref, *, n_exp, gsz):
    @pl.when(pl.program_id(0) == 0)
    def _():
        carry_ref[...] = jnp.zeros_like(carry_ref)

    tm = lg_ref.shape[0]
    lane_i = lax.broadcasted_iota(jnp.int32, (tm, LANES), 1)
    lane = lane_i.astype(F32)
    valid = lane_i < n_exp
    pos = lane_i % gsz
    neg = -jnp.inf
    big = float(LANES)
    scores = _sigmoid(lg_ref[...])
    biased = jnp.where(valid, scores + rb_ref[...], neg)

    def member(s):
        fwd = pltpu.roll(biased, LANES - s, 1)
        back = pltpu.roll(biased, gsz - s, 1)
        return jnp.where(pos + s < gsz, fwd, back)

    xs = [biased] + [member(s) for s in range(1, gsz)]
    gscore = None
    for a in range(gsz):
        for b in range(a + 1, gsz):
            pair = xs[a] + xs[b]
            gscore = pair if gscore is None else jnp.maximum(gscore, pair)
    gscore = jnp.where(valid, gscore, neg)
    gmax = jnp.max(gscore, axis=1, keepdims=True)
    grp = (lane_i // gsz).astype(F32)
    sel = jnp.min(jnp.where(gscore == gmax, grp, big), axis=1, keepdims=True)
    m = jnp.where((grp == sel) & valid, biased, neg)
    m1 = jnp.max(m, axis=1, keepdims=True)
    i1 = jnp.min(jnp.where(m == m1, lane, big), axis=1, keepdims=True)
    m = jnp.where(lane == i1, neg, m)
    m2 = jnp.max(m, axis=1, keepdims=True)
    i2 = jnp.min(jnp.where(m == m2, lane, big), axis=1, keepdims=True)
    oh1 = lane == i1
    oh2 = lane == i2
    s1 = jnp.sum(jnp.where(oh1, scores, 0.0), axis=1, keepdims=True)
    s2 = jnp.sum(jnp.where(oh2, scores, 0.0), axis=1, keepdims=True)
    both = oh1.astype(F32) + oh2.astype(F32)
    ri = lax.broadcasted_iota(jnp.int32, (tm, tm), 0)
    ci = lax.broadcasted_iota(jnp.int32, (tm, tm), 1)
    earlier = (ri > ci).astype(BF16)
    tot = carry_ref[...] + jnp.dot(earlier, both.astype(BF16), preferred_element_type=F32)
    r1 = jnp.sum(jnp.where(oh1, tot, 0.0), axis=1, keepdims=True)
    r2 = jnp.sum(jnp.where(oh2, tot, 0.0), axis=1, keepdims=True)
    carry_ref[...] += jnp.sum(both, axis=0, keepdims=True)
    cnt_ref[...] = carry_ref[...]
    out = jnp.zeros((tm, LANES), F32)
    for k, val in enumerate((i1, i2, r1, r2, s1 / (s1 + s2), s2 / (s1 + s2))):
        out = jnp.where(lane_i == k, val, out)
    o_ref[...] = out


def _route(dm, logits, router_b):
    n_tok = dm.t
    tm = _tile(256, n_tok)
    ne = logits.shape[1]
    rb = jnp.pad(router_b.astype(F32), (0, ne - dm.n_exp)).reshape(1, ne)
    sel, cnt = pl.pallas_call(
        functools.partial(_route_kernel, n_exp=dm.n_exp, gsz=dm.n_exp // ROUTER_GROUPS),
        grid=(n_tok // tm,),
        in_specs=[pl.BlockSpec((tm, ne), lambda i: (i, 0)), pl.BlockSpec((1, ne), lambda i: (0, 0))],
        out_specs=[pl.BlockSpec((tm, ne), lambda i: (i, 0)), pl.BlockSpec((1, ne), lambda i: (0, 0))],
        out_shape=[jax.ShapeDtypeStruct((n_tok, ne), F32), jax.ShapeDtypeStruct((1, ne), F32)],
        scratch_shapes=[pltpu.VMEM((1, ne), F32)],
        compiler_params=_params(("arbitrary",)),
        name="route",
    )(logits, rb)
    top_e = sel[:, 0:TOP_K].astype(jnp.int32)
    rank = sel[:, TOP_K:2 * TOP_K].astype(jnp.int32)
    counts = cnt[0, :dm.n_exp].astype(jnp.int32)
    padded = (counts + MOE_ROWS - 1) // MOE_ROWS * MOE_ROWS
    pend = jnp.cumsum(padded)
    pstart = pend - padded
    slot_dest = pstart[top_e] + rank
    n_slot = n_tok * TOP_K
    n_buf = -(-n_slot // MOE_ROWS) * MOE_ROWS + dm.n_exp * MOE_ROWS
    n_blk = n_buf // MOE_ROWS
    tok = jnp.repeat(jnp.arange(n_tok, dtype=jnp.int32), TOP_K)
    buf_tok = jnp.full((n_buf,), n_tok, dtype=jnp.int32).at[slot_dest.reshape(n_slot)].set(tok)
    blk_e = jnp.minimum(jnp.searchsorted(pend, jnp.arange(n_blk, dtype=jnp.int32) * MOE_ROWS, side='right'),
                        dm.n_exp - 1).astype(jnp.int32)
    n_used = (pend[-1] // MOE_ROWS).astype(jnp.int32).reshape(1)
    return sel, buf_tok, blk_e, slot_dest, n_used


def _row_copy(src_ref, src_row, dst_ref, dst_row, sem):
    return pltpu.make_async_copy(src_ref.at[pl.ds(src_row, 1)], dst_ref.at[pl.ds(dst_row, 1)], sem)


def _gather_kernel(tok_ref, h_ref, xb_ref, sem, *, rows, n_tok):
    base = pl.program_id(0) * rows

    def copy(r):
        tok = tok_ref[base + r]
        return tok < n_tok, _row_copy(h_ref, jnp.minimum(tok, n_tok - 1), xb_ref, r, sem)

    def start(r, carry):
        is_real, real = copy(r)

        @pl.when(is_real)
        def _():
            real.start()

        @pl.when(jnp.logical_not(is_real))
        def _():
            xb_ref[pl.ds(r, 1), :] = jnp.zeros((1, xb_ref.shape[1]), xb_ref.dtype)

        return carry

    def wait(r, carry):
        is_real, real = copy(r)

        @pl.when(is_real)
        def _():
            real.wait()

        return carry

    lax.fori_loop(0, rows, start, 0)
    lax.fori_loop(0, rows, wait, 0)


def _gather_rows(dm, h, buf_tok):
    n_buf = buf_tok.shape[0]
    rows = _tile(512, n_buf)
    return pl.pallas_call(
        functools.partial(_gather_kernel, rows=rows, n_tok=dm.t),
        grid_spec=pltpu.PrefetchScalarGridSpec(
            num_scalar_prefetch=1,
            grid=(n_buf // rows,),
            in_specs=[pl.BlockSpec(memory_space=pl.ANY)],
            out_specs=pl.BlockSpec((rows, dm.d), lambda i, tok: (i, 0)),
            scratch_shapes=[pltpu.SemaphoreType.DMA(())]),
        out_shape=jax.ShapeDtypeStruct((n_buf, dm.d), h.dtype),
        compiler_params=_params(("arbitrary",)),
        name="moe_gather",
    )(buf_tok, h)


def _moe(dm, h, logits, router_b, w_gate, w_up, w_down, l):
    sel, buf_tok, blk_e, slot_dest, n_used = _route(dm, logits, router_b)
    xb = _gather_rows(dm, h, buf_tok)
    yb = _expert_ffn(dm, xb, blk_e, n_used, w_gate, w_up, w_down, l)
    return yb, slot_dest.reshape(dm.t * TOP_K), sel


def _grid_pos_embed(rows, d):
    nf = d // 4
    omega = 1.0 / (POS_BASE ** (jnp.arange(nf, dtype=F32) / nf))
    ang_r = jnp.arange(rows, dtype=F32)[:, None] * omega
    ang_c = jnp.arange(GRID_W, dtype=F32)[:, None] * omega
    emb_r = jnp.concatenate([jnp.sin(ang_r), jnp.cos(ang_r)], -1)
    emb_c = jnp.concatenate([jnp.sin(ang_c), jnp.cos(ang_c)], -1)
    emb = jnp.concatenate([jnp.broadcast_to(emb_r[:, None], (rows, GRID_W, d // 2)),
                           jnp.broadcast_to(emb_c[None], (rows, GRID_W, d // 2))], -1)
    return emb.reshape(rows * GRID_W, d)


def _ssd_gate_layouts(dm, dt, a_log):
    g_n, hpg = SSM_GROUPS, dm.hpg
    dtr = dt.reshape(dm.t, 2, g_n, hpg).transpose(1, 2, 3, 0)
    alr = a_log.astype(F32).reshape(2, g_n, hpg, 1)
    return dtr, alr


def _gdn_gate_layouts(dm, g, beta, cl_list):
    j_n = dm.qkh
    g4 = g.reshape(dm.t, 2, j_n, 2).transpose(2, 0, 1, 3).reshape(j_n, dm.t, 4)
    b4 = beta.reshape(dm.t, 2, j_n, 2).transpose(2, 0, 1, 3).reshape(j_n, dm.t, 4)
    gb = jnp.concatenate([g4, b4], axis=-1)
    grows = []
    for (row_off, n_rows, cl) in cl_list:
        part = gb[:, row_off:row_off + n_rows]
        grows.append(part.reshape(j_n, n_rows // cl, cl, 8).transpose(0, 1, 3, 2))
    return grows


def kernel(x_prompt, x_sample, state_ssm, state_gdn, c, c_ctx, w_ada, b_ada, w_in, conv_ssm_w, conv_ssm_b, ssm_dt_bias, ssm_a_log, ssm_d, ssm_norm_w, conv_gdn_w, gdn_dt_bias, gdn_a_log, gdn_norm_w, w_branch_ssm, w_branch_gdn, w_out, ln1_g, ln1_b, ln2_g, ln2_b, router_w, router_b, w_gate, w_up, w_down):
    dm = _dims(x_prompt, x_sample, state_ssm, w_in, w_gate)
    alpha = (2.0 * dm.depth) ** 0.25

    cond = jnp.zeros((dm.ncond, dm.d), F32).at[0].set(c_ctx).at[1:1 + dm.db].set(c)
    mod4 = _ada_mod(dm, cond, w_ada, b_ada).reshape(dm.depth, dm.ncond, 1, 6 * dm.d)
    pos = _grid_pos_embed(dm.dseq // GRID_W, dm.d)
    x, h = _embed(dm, x_prompt.reshape(dm.tp, dm.d), x_sample.reshape(dm.t - dm.tp, dm.d), pos, mod4)

    rw_pad = jnp.pad(router_w, ((0, 0), (0, -router_w.shape[1] % LANES)))
    h0_ssm = state_ssm.reshape(dm.db, dm.depth, 2, SSM_GROUPS, dm.gw, dm.state)
    cl_ctx, cl_lat = min(SCAN_CHUNK, dm.seq), min(SCAN_CHUNK, dm.dseq)
    ssm_states, gdn_states = [], []
    for l in range(dm.depth):
        proj = _in_proj(dm, h, w_in, l)
        dt, g, beta = _gates(dm, proj[:, dm.off_dt:dm.off_dt + 2 * dm.heads],
                             proj[:, dm.off_a:dm.off_a + 2 * dm.vh], proj[:, dm.off_b:dm.off_b + 2 * dm.vh],
                             ssm_dt_bias[l], gdn_dt_bias[l], gdn_a_log[l])
        act_s = _conv_act(dm, proj, dm.off_xbc, dm.xbc, conv_ssm_w, conv_ssm_b, l)
        dtr, alr = _ssd_gate_layouts(dm, dt, ssm_a_log[l])
        dskip = jnp.repeat(ssm_d[l].astype(F32), SSM_HEADDIM).reshape(1, dm.inner)
        nw_s = ssm_norm_w[l].reshape(1, dm.inner)
        y_ssm, hs = _ssd_call(dm, act_s, proj, dtr, alr, dskip, nw_s, None, None, l, 0, dm.nb, dm.seq)
        (y_ssm,) = _ssd_call(dm, act_s, proj, dtr, alr, dskip, nw_s, y_ssm, h0_ssm, l, dm.tp, dm.db, dm.dseq)
        ssm_states.append(hs.reshape(dm.nb, 2, dm.heads, SSM_HEADDIM, dm.state))
        grow_ctx, grow_lat = _gdn_gate_layouts(dm, g, beta, [(0, dm.tp, cl_ctx), (dm.tp, dm.t - dm.tp, cl_lat)])
        nw_g = gdn_norm_w[l].reshape(1, GDN_DV)
        y_gdn, ss = _gdn_call(dm, proj, conv_gdn_w, grow_ctx, nw_g, None, None, l, 0, dm.nb, dm.seq)
        (y_gdn,) = _gdn_call(dm, proj, conv_gdn_w, grow_lat, nw_g, y_gdn, state_gdn, l, dm.tp, dm.db, dm.dseq)
        gdn_states.append(ss)
        merged = _branch_merge(dm, y_ssm, y_gdn, w_branch_ssm, w_branch_gdn, proj, l)
        pre = _out_proj(dm, merged, w_out, x, mod4, l, alpha)
        x, h2, logits = _ln_router(dm, pre, ln1_g, ln1_b, mod4, rw_pad, l)
        yb, slot_dest, sel = _moe(dm, h2, logits, router_b, w_gate, w_up, w_down, l)
        outs = _ln_res(dm, x, yb, slot_dest, sel, ln2_g, ln2_b, mod4, l, alpha)
        x = outs[0]
        if l + 1 < dm.depth:
            h = outs[1]

    y_prompt = x[:dm.tp].reshape(dm.nb, dm.seq, dm.d)
    y_sample = x[dm.tp:].reshape(dm.db, dm.dseq, dm.d)
    return (y_prompt, y_sample, jnp.stack(ssm_states, axis=1), jnp.stack(gdn_states, axis=1))
```
